```python
import numpy as np
import jax
import jax.numpy as jnp
from jax import lax

D_MODEL = 2048
BATCH = 1
SEQ = 8192
DEPTH = 2

GRID_W = 64
CTX_LEN = 256
HEAD_DIM = 64
N_BRANCH = 4
BRANCH_W = D_MODEL // N_BRANCH
NA_HEADS = BRANCH_W // HEAD_DIM
WIN_ROWS = 8
WIN_COLS = 16
RG_BLOCKS = BRANCH_W // HEAD_DIM
RG_BW = BRANCH_W // RG_BLOCKS
RG_C = 8.0
RG_CONV = 4
CONF_WIDTH = 31
GQA_HEADS = BRANCH_W // HEAD_DIM
GQA_KV_HEADS = 2
GQA_GROUP = GQA_HEADS // GQA_KV_HEADS
Q_BLOCK = 128
ROPE_THETA = 10000.0
D_FF = 256 * ((8 * D_MODEL // 3 + 255) // 256)
FFN_CONV = 3
EPS = 1e-6
SPLIT_SIZES = (BRANCH_W, BRANCH_W, BRANCH_W,
               BRANCH_W, BRANCH_W,
               2 * BRANCH_W,
               GQA_HEADS * HEAD_DIM, GQA_KV_HEADS * HEAD_DIM, GQA_KV_HEADS * HEAD_DIM,
               N_BRANCH * D_MODEL)
N_IN = sum(SPLIT_SIZES)

kernel_name = 'hybrid_natten_rglru_conformer_gqa_dit'


def rmsnorm(x, g):
    xf = x.astype(jnp.float32)
    y = xf * lax.rsqrt(jnp.mean(xf * xf, axis=-1, keepdims=True) + EPS)
    return (y * g.astype(jnp.float32)).astype(x.dtype)


def layernorm(x, g, b):
    xf = x.astype(jnp.float32)
    mu = jnp.mean(xf, axis=-1, keepdims=True)
    xc = xf - mu
    y = xc * lax.rsqrt(jnp.mean(xc * xc, axis=-1, keepdims=True) + EPS)
    return (y * g.astype(jnp.float32) + b.astype(jnp.float32)).astype(x.dtype)


def modulate(h, shift, scale):
    return h * (1.0 + scale[:, None, :]) + shift[:, None, :]


def dwconv(x, w, pad):
    return lax.conv_general_dilated(x, w[:, None, :].astype(x.dtype), (1,), [pad],
                                    dimension_numbers=('NWC', 'WIO', 'NWC'),
                                    feature_group_count=x.shape[-1])


def heads(t, n):
    return t.reshape(t.shape[0], t.shape[1], n, HEAD_DIM)


def axial_rope(x, pos_r, pos_c):
    half = x.shape[-1] // 2
    n_f = half // 2
    inv_freq = ROPE_THETA ** (-jnp.arange(n_f, dtype=jnp.float32) / n_f)

    def rot(xh, pos):
        ang = pos.astype(jnp.float32)[:, None] * inv_freq[None, :]
        cos = jnp.cos(ang)[None, :, None, :].astype(x.dtype)
        sin = jnp.sin(ang)[None, :, None, :].astype(x.dtype)
        x1, x2 = xh[..., :n_f], xh[..., n_f:]
        return jnp.concatenate([x1 * cos - x2 * sin, x2 * cos + x1 * sin], axis=-1)

    return jnp.concatenate([rot(x[..., :half], pos_r), rot(x[..., half:], pos_c)], axis=-1)


def dense_attn(q, k, v):
    bsz, s, hq, dh = q.shape
    hk = k.shape[2]
    qg = q.reshape(bsz, s, hk, hq // hk, dh)
    sc = jnp.einsum('bqkgd,bskd->bkgqs', qg, k).astype(jnp.float32) * dh ** -0.5
    p = jax.nn.softmax(sc, axis=-1).astype(v.dtype)
    o = jnp.einsum('bkgqs,bskd->bqkgd', p, v)
    return o.reshape(bsz, s, hq * dh)


def natten_latent(q, k, v, k_ctx, v_ctx, rpb, rows):
    bsz, seq, n_h, dh = q.shape
    kr = min(WIN_ROWS, rows)
    kc = WIN_COLS
    n_loc = kr * kc
    cols = jnp.arange(GRID_W)
    col_start = jnp.clip(cols - kc // 2, 0, GRID_W - kc)
    key_cols = col_start[:, None] + jnp.arange(kc)[None, :]
    dc = key_cols - cols[:, None] + (WIN_COLS - 1)
    rpb32 = rpb.astype(jnp.float32)
    q_rows = q.reshape(bsz, rows, GRID_W, n_h, dh).transpose(1, 0, 2, 3, 4)
    scale = dh ** -0.5

    def one_row(args):
        r, q_r = args
        row_start = jnp.clip(r - kr // 2, 0, rows - kr)
        key_rows = row_start + jnp.arange(kr)
        idx = (key_rows[None, :, None] * GRID_W + key_cols[:, None, :]).reshape(GRID_W, n_loc)
        k_g = k[:, idx]
        v_g = v[:, idx]
        dr = key_rows - r + (WIN_ROWS - 1)
        bias = rpb32[:, dr[None, :, None], dc[:, None, :]].reshape(n_h, GRID_W, n_loc)
        s_loc = jnp.einsum('bqhd,bqkhd->bhqk', q_r, k_g).astype(jnp.float32) * scale + bias[None]
        s_ctx = jnp.einsum('bqhd,bchd->bhqc', q_r, k_ctx).astype(jnp.float32) * scale
        p = jax.nn.softmax(jnp.concatenate([s_loc, s_ctx], axis=-1), axis=-1).astype(v.dtype)
        return (jnp.einsum('bhqk,bqkhd->bqhd', p[..., :n_loc], v_g)
                + jnp.einsum('bhqc,bchd->bqhd', p[..., n_loc:], v_ctx))

    out = lax.map(one_row, (jnp.arange(rows), q_rows))
    return out.transpose(1, 0, 2, 3, 4).reshape(bsz, seq, n_h * dh)


def gqa_latent(q, k_all, v_all):
    bsz, seq, n_h, dh = q.shape
    n_blk = seq // Q_BLOCK
    qb = q.reshape(bsz, n_blk, Q_BLOCK, GQA_KV_HEADS, GQA_GROUP, dh).transpose(1, 0, 2, 3, 4, 5)
    scale = dh ** -0.5

    def one_block(q_blk):
        sc = jnp.einsum('bqkgd,bskd->bkgqs', q_blk, k_all).astype(jnp.float32) * scale
        p = jax.nn.softmax(sc, axis=-1).astype(v_all.dtype)
        return jnp.einsum('bkgqs,bskd->bqkgd', p, v_all)

    o = lax.map(one_block, qb)
    return o.transpose(1, 0, 2, 3, 4, 5).reshape(bsz, seq, n_h * dh)


def rglru_gates(x, w_rg, b_rg, rg_lambda):
    bsz, s, _ = x.shape
    xb = x.reshape(bsz, s, RG_BLOCKS, RG_BW)
    g = jnp.einsum('bsni,dgnij->dgbsnj', xb, w_rg).reshape(2, 2, bsz, s, BRANCH_W)
    g = jax.nn.sigmoid((g + b_rg[:, :, None, None, :]).astype(jnp.float32))
    log_a = -RG_C * g[:, 0] * jax.nn.softplus(-rg_lambda.astype(jnp.float32))[:, None, None, :]
    a = jnp.exp(log_a)
    b = jnp.sqrt(-jnp.expm1(2.0 * log_a)) * g[:, 1] * x.astype(jnp.float32)[None]
    return a, b


def linear_scan(a, b, h0):
    def comb(lhs, rhs):
        return lhs[0] * rhs[0], rhs[0] * lhs[1] + rhs[1]
    a_cum, b_cum = lax.associative_scan(comb, (a, b), axis=1)
    return a_cum * h0[:, None, :] + b_cum


def rglru_branch(x_lat, gate_lat, x_ctx, gate_ctx, rg_conv, w_rg, b_rg, rg_lambda, need_ctx):
    pad = (RG_CONV // 2, RG_CONV - 1 - RG_CONV // 2)
    xl = dwconv(x_lat, rg_conv, pad)
    xc = dwconv(x_ctx, rg_conv, pad)
    a_c, b_c = rglru_gates(xc, w_rg, b_rg, rg_lambda)
    a_l, b_l = rglru_gates(xl, w_rg, b_rg, rg_lambda)
    h0 = jnp.zeros((x_ctx.shape[0], BRANCH_W), jnp.float32)
    hc_f = linear_scan(a_c[0], b_c[0], h0)
    hc_b = linear_scan(a_c[1][:, ::-1], b_c[1][:, ::-1], h0)
    hl_f = linear_scan(a_l[0], b_l[0], hc_f[:, -1])
    hl_b = linear_scan(a_l[1][:, ::-1], b_l[1][:, ::-1], hc_b[:, -1])
    y_lat = (hl_f + hl_b[:, ::-1]).astype(x_lat.dtype) * jax.nn.gelu(gate_lat)
    y_ctx = None
    if need_ctx:
        y_ctx = (hc_f + hc_b[:, ::-1]).astype(x_ctx.dtype) * jax.nn.gelu(gate_ctx)
    return y_lat, y_ctx


def conformer_branch(u, conf_dw, conf_ln_g, conf_ln_b):
    val, gt = jnp.split(u, 2, axis=-1)
    y = val * jax.nn.sigmoid(gt)
    y = dwconv(y, conf_dw, (CONF_WIDTH // 2, CONF_WIDTH // 2))
    return jax.nn.silu(layernorm(y, conf_ln_g, conf_ln_b))


def merge(branches, gate_logits, w_branch, w_out):
    bsz, s, _ = gate_logits.shape
    gates = jax.nn.sigmoid(gate_logits.reshape(bsz, s, N_BRANCH, D_MODEL))
    y = gates[:, :, 0] * (branches[0] @ w_branch[0])
    for n in range(1, N_BRANCH):
        y = y + gates[:, :, n] * (branches[n] @ w_branch[n])
    return y @ w_out


def mixer_sublayer(h_lat, h_ctx, w_in, na_rpb, rg_conv, w_rg, b_rg, rg_lambda, conf_dw, conf_ln_g,
                   conf_ln_b, q_norm_g, k_norm_g, w_branch, w_out, pos_r, pos_c, rows, need_ctx):
    split_idx = [int(v) for v in np.cumsum(SPLIT_SIZES)[:-1]]
    zl = jnp.split(h_lat @ w_in, split_idx, axis=-1)
    zc = jnp.split(h_ctx @ w_in, split_idx, axis=-1)
    qa, ka, va = heads(zl[0], NA_HEADS), heads(zl[1], NA_HEADS), heads(zl[2], NA_HEADS)
    qac, kac, vac = heads(zc[0], NA_HEADS), heads(zc[1], NA_HEADS), heads(zc[2], NA_HEADS)
    a_lat = natten_latent(qa, ka, va, kac, vac, na_rpb, rows)
    b_lat, b_ctx = rglru_branch(zl[3], zl[4], zc[3], zc[4], rg_conv, w_rg, b_rg, rg_lambda, need_ctx)
    c_lat = conformer_branch(zl[5], conf_dw, conf_ln_g, conf_ln_b)
    qd = axial_rope(rmsnorm(heads(zl[6], GQA_HEADS), q_norm_g), pos_r, pos_c)
    kd = axial_rope(rmsnorm(heads(zl[7], GQA_KV_HEADS), k_norm_g), pos_r, pos_c)
    vd = heads(zl[8], GQA_KV_HEADS)
    qdc = rmsnorm(heads(zc[6], GQA_HEADS), q_norm_g)
    kdc = rmsnorm(heads(zc[7], GQA_KV_HEADS), k_norm_g)
    vdc = heads(zc[8], GQA_KV_HEADS)
    d_lat = gqa_latent(qd, jnp.concatenate([kd, kdc], axis=1), jnp.concatenate([vd, vdc], axis=1))
    out_lat = merge([a_lat, b_lat, c_lat, d_lat], zl[9], w_branch, w_out)
    out_ctx = None
    if need_ctx:
        a_ctx = dense_attn(qac, kac, vac)
        c_ctx_b = conformer_branch(zc[5], conf_dw, conf_ln_g, conf_ln_b)
        d_ctx = dense_attn(qdc, kdc, vdc)
        out_ctx = merge([a_ctx, b_ctx, c_ctx_b, d_ctx], zc[9], w_branch, w_out)
    return out_lat, out_ctx


def conv_ffn(h, w_up, ffn_dw, w_down):
    u = dwconv(h @ w_up, ffn_dw, (FFN_CONV // 2, FFN_CONV // 2))
    g, v = jnp.split(u, 2, axis=-1)
    return (jax.nn.silu(g) * v) @ w_down


def setup_inputs(seed: int = 0) -> dict:
    key = jax.random.key(seed)
    ks = jax.random.split(key, 26)
    f32 = jnp.float32

    def nrm(k, shape, scale):
        return jax.random.normal(k, shape, f32) * scale

    a0 = jax.random.uniform(ks[12], (DEPTH, 2, BRANCH_W), f32, 0.9, 0.999)
    return {
        'x': nrm(ks[0], (BATCH, SEQ, D_MODEL), 1.0),
        'c': nrm(ks[1], (BATCH, D_MODEL), 1.0),
        'ctx': nrm(ks[2], (BATCH, CTX_LEN, D_MODEL), 1.0),
        'c_ctx': nrm(ks[3], (D_MODEL,), 1.0),
        'w_ada': nrm(ks[4], (DEPTH, D_MODEL, 6 * D_MODEL), 0.5 * D_MODEL ** -0.5),
        'b_ada': nrm(ks[5], (DEPTH, 6 * D_MODEL), 0.01),
        'g_mix': 1.0 + nrm(ks[6], (DEPTH, D_MODEL), 0.05),
        'w_in': nrm(ks[7], (DEPTH, D_MODEL, N_IN), D_MODEL ** -0.5),
        'na_rpb': nrm(ks[8], (DEPTH, NA_HEADS, 2 * WIN_ROWS - 1, 2 * WIN_COLS - 1), 0.1),
        'rg_conv': nrm(ks[9], (DEPTH, RG_CONV, BRANCH_W), RG_CONV ** -0.5),
        'w_rg': nrm(ks[10], (DEPTH, 2, 2, RG_BLOCKS, RG_BW, RG_BW), RG_BW ** -0.5),
        'b_rg': nrm(ks[11], (DEPTH, 2, 2, BRANCH_W), 0.1),
        'rg_lambda': jnp.log(a0) - jnp.log1p(-a0),
        'conf_dw': nrm(ks[13], (DEPTH, CONF_WIDTH, BRANCH_W), CONF_WIDTH ** -0.5),
        'conf_ln_g': 1.0 + nrm(ks[14], (DEPTH, BRANCH_W), 0.05),
        'conf_ln_b': nrm(ks[15], (DEPTH, BRANCH_W), 0.05),
        'q_norm_g': 1.0 + nrm(ks[16], (DEPTH, HEAD_DIM), 0.05),
        'k_norm_g': 1.0 + nrm(ks[17], (DEPTH, HEAD_DIM), 0.05),
        'w_branch': nrm(ks[18], (DEPTH, N_BRANCH, BRANCH_W, D_MODEL), BRANCH_W ** -0.5),
        'w_out': nrm(ks[19], (DEPTH, D_MODEL, D_MODEL), D_MODEL ** -0.5),
        'g_ffn': 1.0 + nrm(ks[20], (DEPTH, D_MODEL), 0.05),
        'w_up': nrm(ks[21], (DEPTH, D_MODEL, 2 * D_FF), D_MODEL ** -0.5),
        'ffn_dw': nrm(ks[22], (DEPTH, FFN_CONV, 2 * D_FF), FFN_CONV ** -0.5),
        'w_down': nrm(ks[23], (DEPTH, D_FF, D_MODEL), D_FF ** -0.5),
        'g_final': 1.0 + nrm(ks[24], (D_MODEL,), 0.05),
    }


def reference(x, c, ctx, c_ctx, w_ada, b_ada, g_mix, w_in, na_rpb, rg_conv, w_rg, b_rg, rg_lambda,
              conf_dw, conf_ln_g, conf_ln_b, q_norm_g, k_norm_g, w_branch, w_out, g_ffn, w_up,
              ffn_dw, w_down, g_final):
    seq = x.shape[1]
    rows = seq // GRID_W
    pos = jnp.arange(seq)
    pos_r = pos // GRID_W
    pos_c = pos % GRID_W
    silu_c = jax.nn.silu(c)
    silu_cc = jax.nn.silu(c_ctx)[None]
    for l in range(DEPTH):
        need_ctx = l < DEPTH - 1
        mod_l = jnp.split(silu_c @ w_ada[l] + b_ada[l], 6, axis=-1)
        mod_c = jnp.split(silu_cc @ w_ada[l] + b_ada[l], 6, axis=-1)
        h_l = modulate(rmsnorm(x, g_mix[l]), mod_l[0], mod_l[1])
        h_c = modulate(rmsnorm(ctx, g_mix[l]), mod_c[0], mod_c[1])
        y_l, y_c = mixer_sublayer(h_l, h_c, w_in[l], na_rpb[l], rg_conv[l], w_rg[l], b_rg[l], rg_lambda[l],
                                  conf_dw[l], conf_ln_g[l], conf_ln_b[l], q_norm_g[l], k_norm_g[l],
                                  w_branch[l], w_out[l], pos_r, pos_c, rows, need_ctx)
        x = x + mod_l[2][:, None, :] * y_l
        x = x + mod_l[5][:, None, :] * conv_ffn(modulate(rmsnorm(x, g_ffn[l]), mod_l[3], mod_l[4]),
                                                w_up[l], ffn_dw[l], w_down[l])
        if need_ctx:
            ctx = ctx + mod_c[2][:, None, :] * y_c
            ctx = ctx + mod_c[5][:, None, :] * conv_ffn(modulate(rmsnorm(ctx, g_ffn[l]), mod_c[3], mod_c[4]),
                                                        w_up[l], ffn_dw[l], w_down[l])
    return rmsnorm(x, g_final)
```

```python
import functools

import numpy as np
import jax
import jax.numpy as jnp
from jax import lax
from jax.experimental import pallas as pl
from jax.experimental.pallas import tpu as pltpu

F32 = jnp.float32
BF16 = jnp.bfloat16

D_MODEL = 2048
DEPTH = 2
GRID_W = 64
HEAD_DIM = 64
N_BRANCH = 4
BRANCH_W = D_MODEL // N_BRANCH
WIN_ROWS = 8
WIN_COLS = 16
RG_C = 8.0
RG_CONV = 4
CONF_WIDTH = 31
GQA_KV_HEADS = 2
ROPE_THETA = 10000.0
D_FF = 5632
EPS = 1e-6
N_MOD = 6
N_GATE = N_BRANCH * D_MODEL
N_REST = 3 * BRANCH_W + 2 * BRANCH_W + 2 * BRANCH_W + BRANCH_W + 2 * GQA_KV_HEADS * HEAD_DIM
N_IN = N_GATE + N_REST

ROW_TILE = 256
HALO = 16
NA_ROWS = ROW_TILE // GRID_W
NA_KEY_ROWS = 3 * NA_ROWS
NEG = -1e30
VMEM_LIMIT = 56 * 1024 * 1024

COL_QA, COL_KA, COL_VA, COL_XR, COL_GR, COL_CV, COL_CG, COL_QD = (N_GATE // BRANCH_W + n for n in range(8))
COL_KD = (N_GATE + 8 * BRANCH_W) // 128
COL_VD = COL_KD + 1


def _params(*sem):
    return pltpu.CompilerParams(dimension_semantics=sem, vmem_limit_bytes=VMEM_LIMIT)


def _pick(total, candidates):
    for cand in candidates:
        if total % cand == 0:
            return cand
    raise ValueError(f"no tile for {total} in {candidates}")


def _dot(a, b):
    return jnp.dot(a, b, preferred_element_type=F32)


def _dot_nt(a, b):
    return lax.dot_general(a, b, (((1,), (1,)), ((), ())), preferred_element_type=F32)


def _sigmoid(x):
    return jax.nn.sigmoid(x)


def _ada_kernel(cc_ref, w_ref, b_ref, o_ref):
    a = cc_ref[...]
    a = a * _sigmoid(a)
    o_ref[0] = _dot(a.astype(BF16), w_ref[0].astype(BF16)) + b_ref[0]


def _ada(cc, w_ada, b_ada):
    depth, d, n = w_ada.shape
    tn = 1024
    return pl.pallas_call(
        _ada_kernel,
        grid=(depth, n // tn),
        in_specs=[pl.BlockSpec((8, d), lambda l, j: (0, 0)),
                  pl.BlockSpec((1, d, tn), lambda l, j: (l, 0, j)),
                  pl.BlockSpec((1, 1, tn), lambda l, j: (l, 0, j))],
        out_specs=pl.BlockSpec((1, 8, tn), lambda l, j: (l, 0, j)),
        out_shape=jax.ShapeDtypeStruct((depth, 8, n), F32),
        compiler_params=_params("parallel", "parallel"),
        name="ada",
    )(cc, w_ada, b_ada.reshape(depth, 1, n))


def _proj_kernel(x_ref, g_ref, mod_ref, w_ref, o_ref, h_ref, *, tm, n_ctx, shift_row):
    i = pl.program_id(0)

    @pl.when(pl.program_id(1) == 0)
    def _():
        x = x_ref[...]
        y = x * lax.rsqrt(jnp.mean(x * x, axis=-1, keepdims=True) + EPS) * g_ref[...]
        rows = i * tm + lax.broadcasted_iota(jnp.int32, (tm, 1), 0)
        is_ctx = rows < n_ctx
        shift = jnp.where(is_ctx, mod_ref[0, 1, shift_row:shift_row + 1, :], mod_ref[0, 0, shift_row:shift_row + 1, :])
        scale = jnp.where(is_ctx, mod_ref[0, 1, shift_row + 1:shift_row + 2, :],
                          mod_ref[0, 0, shift_row + 1:shift_row + 2, :])
        h_ref[...] = (y * (1.0 + scale) + shift).astype(BF16)

    o_ref[...] = _dot(h_ref[...], w_ref[...]).astype(o_ref.dtype)


def _proj(x, g, mod, layer, shift_row, w, n_ctx, tn_candidates):
    t, d = x.shape
    n = w.shape[1]
    tm = _pick(t, (768, 256))
    tn = _pick(n, tn_candidates)
    return pl.pallas_call(
        functools.partial(_proj_kernel, tm=tm, n_ctx=n_ctx, shift_row=shift_row),
        grid=(t // tm, n // tn),
        in_specs=[pl.BlockSpec((tm, d), lambda i, j: (i, 0)),
                  pl.BlockSpec((1, d), lambda i, j: (0, 0)),
                  pl.BlockSpec((1, 2, 8, d), lambda i, j: (layer, 0, 0, 0)),
                  pl.BlockSpec((d, tn), lambda i, j: (0, j))],
        out_specs=pl.BlockSpec((tm, tn), lambda i, j: (i, j)),
        out_shape=jax.ShapeDtypeStruct((t, n), BF16),
        scratch_shapes=[pltpu.VMEM((tm, d), BF16)],
        compiler_params=_params("parallel", "arbitrary"),
        name="proj",
    )(x, g.reshape(1, d), mod, w)


def _softplus(x):
    return jnp.maximum(x, 0.0) + jnp.log(1.0 + jnp.exp(-jnp.abs(x)))


def _scan_rows(a, b, reverse):
    n = a.shape[0]
    row = lax.broadcasted_iota(jnp.int32, a.shape, 0)
    d = 1
    while d < n:
        if reverse:
            keep = row < n - d
            shift = n - d
        else:
            keep = row >= d
            shift = d
        a_sh = jnp.where(keep, pltpu.roll(a, shift, 0), 1.0)
        b_sh = jnp.where(keep, pltpu.roll(b, shift, 0), 0.0)
        b = a * b_sh + b
        a = a * a_sh
        d *= 2
    return a, b


def _rg_kernel(xfp, xfm, xfn, xbp, xbm, xbn, cw_ref, wbd_ref, brg_ref, lam_ref, hf_ref, hb_ref,
               ext, carry_f, carry_b, *, nchunks):
    s = pl.program_id(0)
    tm = xfm.shape[0]
    bw = xfm.shape[1]

    @pl.when(s == 0)
    def _():
        carry_f[...] = jnp.zeros_like(carry_f)
        carry_b[...] = jnp.zeros_like(carry_b)

    cb = jnp.where(s == 0, 0, nchunks - s)

    def gates(d, xp, xm, xn, cidx):
        prev_ok = (cidx >= 2).astype(F32)
        next_ok = jnp.logical_and(cidx >= 1, cidx <= nchunks - 2).astype(F32)
        ext[0:HALO, :] = xp[...].astype(F32) * prev_ok
        ext[HALO:HALO + tm, :] = xm[...].astype(F32)
        ext[HALO + tm:2 * HALO + tm, :] = xn[...].astype(F32) * next_ok
        left = RG_CONV // 2
        xl = cw_ref[0:1, :] * ext[HALO - left:HALO - left + tm, :]
        for k in range(1, RG_CONV):
            xl = xl + cw_ref[k:k + 1, :] * ext[HALO - left + k:HALO - left + k + tm, :]
        g = _sigmoid(_dot(xl.astype(BF16), wbd_ref[d]) + brg_ref[d])
        log_a = (-RG_C * g[:, :bw]) * _softplus(-lam_ref[d])
        a = jnp.exp(log_a)
        b = jnp.sqrt(-jnp.tanh(log_a) * (a * a + 1.0)) * g[:, bw:] * xl
        return a, b

    a, b = gates(0, xfp, xfm, xfn, s)
    a, b = _scan_rows(a, b, reverse=False)
    h = b + a * carry_f[0:1, :]
    hf_ref[...] = h
    carry_f[...] = jnp.broadcast_to(h[tm - 1:tm, :], carry_f.shape)

    a, b = gates(1, xbp, xbm, xbn, cb)
    a, b = _scan_rows(a, b, reverse=True)
    h = b + a * carry_b[0:1, :]
    hb_ref[...] = h
    carry_b[...] = jnp.broadcast_to(h[0:1, :], carry_b.shape)


def _rglru(z, rg_conv, wbd, brg, lam):
    t = z.shape[0]
    tm = ROW_TILE
    nchunks = t // tm
    hpt = tm // HALO
    nh = t // HALO

    def bwd(s):
        return jnp.where(s == 0, 0, nchunks - s)

    def prev_map(cidx):
        return lambda s: (jnp.maximum(cidx(s) * hpt - 1, 0), COL_XR)

    def next_map(cidx):
        return lambda s: (jnp.minimum((cidx(s) + 1) * hpt, nh - 1), COL_XR)

    fwd = lambda s: s
    in_specs = [
        pl.BlockSpec((HALO, BRANCH_W), prev_map(fwd)),
        pl.BlockSpec((tm, BRANCH_W), lambda s: (s, COL_XR)),
        pl.BlockSpec((HALO, BRANCH_W), next_map(fwd)),
        pl.BlockSpec((HALO, BRANCH_W), prev_map(bwd)),
        pl.BlockSpec((tm, BRANCH_W), lambda s: (bwd(s), COL_XR)),
        pl.BlockSpec((HALO, BRANCH_W), next_map(bwd)),
        pl.BlockSpec((RG_CONV, BRANCH_W), lambda s: (0, 0)),
        pl.BlockSpec((2, BRANCH_W, 2 * BRANCH_W), lambda s: (0, 0, 0)),
        pl.BlockSpec((2, 1, 2 * BRANCH_W), lambda s: (0, 0, 0)),
        pl.BlockSpec((2, 1, BRANCH_W), lambda s: (0, 0, 0)),
    ]
    return pl.pallas_call(
        functools.partial(_rg_kernel, nchunks=nchunks),
        grid=(nchunks,),
        in_specs=in_specs,
        out_specs=[pl.BlockSpec((tm, BRANCH_W), lambda s: (s, 0)),
                   pl.BlockSpec((tm, BRANCH_W), lambda s: (bwd(s), 0))],
        out_shape=[jax.ShapeDtypeStruct((t, BRANCH_W), F32)] * 2,
        scratch_shapes=[pltpu.VMEM((tm + 2 * HALO, BRANCH_W), F32),
                        pltpu.VMEM((8, BRANCH_W), F32),
                        pltpu.VMEM((8, BRANCH_W), F32)],
        compiler_params=_params("arbitrary"),
        name="rglru",
    )(z, z, z, z, z, z, rg_conv, wbd, brg, lam)


def _conf_kernel(vp, vm, vn, gp, gm, gn, dw_ref, lng_ref, lnb_ref, o_ref, ext, *, nchunks):
    i = pl.program_id(0)
    tm = vm.shape[0]
    prev_ok = (i >= 2).astype(F32)
    next_ok = jnp.logical_and(i >= 1, i <= nchunks - 2).astype(F32)

    def glu(v, g):
        return v[...].astype(F32) * _sigmoid(g[...].astype(F32))

    ext[0:HALO, :] = glu(vp, gp) * prev_ok
    ext[HALO:HALO + tm, :] = glu(vm, gm)
    ext[HALO + tm:2 * HALO + tm, :] = glu(vn, gn) * next_ok
    base = HALO - CONF_WIDTH // 2
    acc = dw_ref[0:1, :] * ext[base:base + tm, :]
    for k in range(1, CONF_WIDTH):
        acc = acc + dw_ref[k:k + 1, :] * ext[base + k:base + k + tm, :]
    mu = jnp.mean(acc, axis=-1, keepdims=True)
    xc = acc - mu
    y = xc * lax.rsqrt(jnp.mean(xc * xc, axis=-1, keepdims=True) + EPS) * lng_ref[...] + lnb_ref[...]
    o_ref[...] = (y * _sigmoid(y)).astype(o_ref.dtype)


def _conformer(z, conf_dw, ln_g, ln_b):
    t = z.shape[0]
    tm = ROW_TILE
    nchunks = t // tm
    hpt = tm // HALO
    nh = t // HALO

    def specs(col):
        return [pl.BlockSpec((HALO, BRANCH_W), lambda i: (jnp.maximum(i * hpt - 1, 0), col)),
                pl.BlockSpec((tm, BRANCH_W), lambda i: (i, col)),
                pl.BlockSpec((HALO, BRANCH_W), lambda i: (jnp.minimum((i + 1) * hpt, nh - 1), col))]

    return pl.pallas_call(
        functools.partial(_conf_kernel, nchunks=nchunks),
        grid=(nchunks,),
        in_specs=specs(COL_CV) + specs(COL_CG) + [
            pl.BlockSpec((CONF_WIDTH, BRANCH_W), lambda i: (0, 0)),
            pl.BlockSpec((1, BRANCH_W), lambda i: (0, 0)),
            pl.BlockSpec((1, BRANCH_W), lambda i: (0, 0))],
        out_specs=pl.BlockSpec((tm, BRANCH_W), lambda i: (i, 0)),
        out_shape=jax.ShapeDtypeStruct((t, BRANCH_W), BF16),
        scratch_shapes=[pltpu.VMEM((tm + 2 * HALO, BRANCH_W), F32)],
        compiler_params=_params("parallel"),
        name="conformer",
    )(z, z, z, z, z, z, conf_dw, ln_g.reshape(1, -1), ln_b.reshape(1, -1))


def _na_bias_table(rpb):
    i = np.arange(NA_ROWS)[:, None]
    j = np.arange(NA_KEY_ROWS)[None, :]
    variants = [(np.zeros_like(i), i), (i, NA_ROWS + i), (np.full_like(i, NA_ROWS), 2 * NA_ROWS + i)]
    c = np.arange(GRID_W)[:, None]
    kc = np.arange(GRID_W)[None, :]
    col_start = np.clip(c - WIN_COLS // 2, 0, GRID_W - WIN_COLS)
    col_ok = (kc >= col_start) & (kc < col_start + WIN_COLS)
    dc = np.clip(kc - c + WIN_COLS - 1, 0, 2 * WIN_COLS - 2)
    tables = [jnp.full((rpb.shape[0], ROW_TILE, NA_KEY_ROWS * GRID_W), NEG, F32)]
    for start, qrow in variants:
        row_ok = (j >= start) & (j < start + WIN_ROWS)
        dr = np.clip(j - qrow + WIN_ROWS - 1, 0, 2 * WIN_ROWS - 2)
        vals = rpb.astype(F32)[:, dr[:, None, :, None], dc[None, :, None, :]]
        ok = row_ok[:, None, :, None] & col_ok[None, :, None, :]
        tables.append(jnp.where(ok[None], vals, NEG).reshape(rpb.shape[0], ROW_TILE, NA_KEY_ROWS * GRID_W))
    return jnp.stack(tables)


def _na_kernel(q_ref, k0, k1, k2, kc, v0, v1, v2, vc, bias_ref, o_ref):
    tm = q_ref.shape[0]
    lane = lax.broadcasted_iota(jnp.int32, (1, 128), 1)
    scale = HEAD_DIM ** -0.5
    qmask = [jnp.where(lane < HEAD_DIM, scale, 0.0).astype(BF16), jnp.where(lane >= HEAD_DIM, scale, 0.0).astype(BF16)]
    first_half = lax.broadcasted_iota(jnp.int32, (tm, 128), 1) < HEAD_DIM
    for p in range(BRANCH_W // 128):
        cols = slice(128 * p, 128 * (p + 1))
        q = q_ref[:, cols]
        ks = [r[:, cols] for r in (k0, k1, k2, kc)]
        vs = [r[:, cols] for r in (v0, v1, v2, vc)]
        outs = []
        for e in range(2):
            qm = q * qmask[e]
            s = [_dot_nt(qm, k) for k in ks]
            for jb in range(3):
                s[jb] = s[jb] + bias_ref[0, 2 * p + e, :, tm * jb:tm * (jb + 1)]
            m = jnp.max(jnp.maximum(jnp.maximum(s[0], s[1]), jnp.maximum(s[2], s[3])), axis=-1, keepdims=True)
            ps = [jnp.exp(sj - m) for sj in s]
            l = jnp.sum(ps[0] + ps[1] + ps[2] + ps[3], axis=-1, keepdims=True)
            o = _dot(ps[0].astype(BF16), vs[0])
            for jb in range(1, 4):
                o = o + _dot(ps[jb].astype(BF16), vs[jb])
            outs.append(o * (1.0 / l))
        o_ref[:, cols] = jnp.where(first_half, outs[0], outs[1]).astype(o_ref.dtype)


def _natten(z, bias):
    t = z.shape[0]
    tm = ROW_TILE
    nt = t // tm
    assert nt >= 4

    def kbase(i):
        return jnp.clip(i - 1, 1, nt - 3)

    def variant(i):
        return jnp.where(i == 0, 0, jnp.where(i == 1, 1, jnp.where(i == nt - 1, 3, 2)))

    def kv_specs(col):
        return [pl.BlockSpec((tm, BRANCH_W), lambda i: (kbase(i), col)),
                pl.BlockSpec((tm, BRANCH_W), lambda i: (kbase(i) + 1, col)),
                pl.BlockSpec((tm, BRANCH_W), lambda i: (kbase(i) + 2, col)),
                pl.BlockSpec((tm, BRANCH_W), lambda i: (0, col))]

    nh = bias.shape[1]
    return pl.pallas_call(
        _na_kernel,
        grid=(nt,),
        in_specs=[pl.BlockSpec((tm, BRANCH_W), lambda i: (i, COL_QA))] + kv_specs(COL_KA) + kv_specs(COL_VA) + [
            pl.BlockSpec((1, nh, tm, 3 * tm), lambda i: (variant(i), 0, 0, 0))],
        out_specs=pl.BlockSpec((tm, BRANCH_W), lambda i: (i, 0)),
        out_shape=jax.ShapeDtypeStruct((t, BRANCH_W), BF16),
        compiler_params=_params("parallel"),
        name="natten",
    )(z, z, z, z, z, z, z, z, z, bias)


def _rope_tables(seq, n_ctx):
    half = HEAD_DIM // 2
    n_f = half // 2
    inv_freq = ROPE_THETA ** (-jnp.arange(n_f, dtype=F32) / n_f)
    pos = jnp.arange(seq)
    ang_r = (pos // GRID_W).astype(F32)[:, None] * inv_freq[None, :]
    ang_c = (pos % GRID_W).astype(F32)[:, None] * inv_freq[None, :]
    cos = jnp.concatenate([jnp.cos(ang_r), jnp.cos(ang_r), jnp.cos(ang_c), jnp.cos(ang_c)], axis=-1)
    sin = jnp.concatenate([-jnp.sin(ang_r), jnp.sin(ang_r), -jnp.sin(ang_c), jnp.sin(ang_c)], axis=-1)
    cos = jnp.concatenate([jnp.ones((n_ctx, HEAD_DIM), F32), cos], axis=0)
    sin = jnp.concatenate([jnp.zeros((n_ctx, HEAD_DIM), F32), sin], axis=0)
    return jnp.tile(cos, (1, 2)), jnp.tile(sin, (1, 2))


def _qkprep_kernel(q_ref, k_ref, v_ref, cos_ref, sin_ref, qg_ref, kg_ref, aq_ref, ak_ref, qo_ref, ko_ref, vo_ref):
    n_f = HEAD_DIM // 4

    def norm_rope(x, g, avg, cos, sin):
        n = x.shape[1]
        x2 = x * x
        hi = x2.astype(BF16)
        lo = (x2 - hi.astype(F32)).astype(BF16)
        ms = _dot(hi, avg) + _dot(lo, avg)
        y = x * lax.rsqrt(ms + EPS) * g
        lane = lax.broadcasted_iota(jnp.int32, x.shape, 1) % (2 * n_f)
        partner = jnp.where(lane < n_f, pltpu.roll(y, n - n_f, 1), pltpu.roll(y, n_f, 1))
        return y * cos + partner * sin

    def dup_heads(x):
        swapped = pltpu.roll(x, HEAD_DIM, 1)
        first = lax.broadcasted_iota(jnp.int32, x.shape, 1) < HEAD_DIM
        return jnp.concatenate([jnp.where(first, x, swapped), jnp.where(first, swapped, x)], axis=1)

    cos = cos_ref[...]
    sin = sin_ref[...]
    reps = q_ref.shape[1] // 128
    q = norm_rope(q_ref[...].astype(F32), qg_ref[...], aq_ref[...],
                  jnp.concatenate([cos] * reps, axis=1), jnp.concatenate([sin] * reps, axis=1))
    qo_ref[...] = (q * HEAD_DIM ** -0.5).astype(BF16)
    k = norm_rope(k_ref[...].astype(F32), kg_ref[...], ak_ref[...], cos, sin)
    ko_ref[...] = dup_heads(k).astype(BF16)
    vo_ref[...] = dup_heads(v_ref[...].astype(F32)).astype(BF16)


def _head_avg(n):
    idx = np.arange(n) // HEAD_DIM
    return jnp.asarray((idx[:, None] == idx[None, :]).astype(np.float32) / HEAD_DIM, BF16)


def _qkprep(z, cos, sin, q_norm_g, k_norm_g):
    t = z.shape[0]
    tm = ROW_TILE
    kvw = GQA_KV_HEADS * HEAD_DIM
    qg = jnp.tile(q_norm_g.astype(F32), BRANCH_W // HEAD_DIM).reshape(1, BRANCH_W)
    kg = jnp.tile(k_norm_g.astype(F32), GQA_KV_HEADS).reshape(1, kvw)
    const = lambda i: (0, 0)
    return pl.pallas_call(
        _qkprep_kernel,
        grid=(t // tm,),
        in_specs=[pl.BlockSpec((tm, BRANCH_W), lambda i: (i, COL_QD)),
                  pl.BlockSpec((tm, kvw), lambda i: (i, COL_KD)),
                  pl.BlockSpec((tm, kvw), lambda i: (i, COL_VD)),
                  pl.BlockSpec((tm, kvw), lambda i: (i, 0)),
                  pl.BlockSpec((tm, kvw), lambda i: (i, 0)),
                  pl.BlockSpec((1, BRANCH_W), const),
                  pl.BlockSpec((1, kvw), const),
                  pl.BlockSpec((BRANCH_W, BRANCH_W), const),
                  pl.BlockSpec((kvw, kvw), const)],
        out_specs=[pl.BlockSpec((tm, BRANCH_W), lambda i: (i, 0)),
                   pl.BlockSpec((tm, 2 * kvw), lambda i: (i, 0)),
                   pl.BlockSpec((tm, 2 * kvw), lambda i: (i, 0))],
        out_shape=[jax.ShapeDtypeStruct((t, BRANCH_W), BF16),
                   jax.ShapeDtypeStruct((t, 2 * kvw), BF16),
                   jax.ShapeDtypeStruct((t, 2 * kvw), BF16)],
        compiler_params=_params("parallel"),
        name="qkprep",
    )(z, z, z, cos, sin, qg, kg, _head_avg(BRANCH_W), _head_avg(kvw))


def _gqa_kernel(q_ref, k_ref, v_ref, o_ref, qs_ref, m_ref, l_ref, acc_ref, *, n_ctx, n_chunks, tk):
    i = pl.program_id(0)
    tq = q_ref.shape[0]
    lane = lax.broadcasted_iota(jnp.int32, (1, 128), 1)
    hmask = [(lane < HEAD_DIM).astype(BF16), (lane >= HEAD_DIM).astype(BF16)]
    first_half = lax.broadcasted_iota(jnp.int32, (tq, 128), 1) < HEAD_DIM
    group = BRANCH_W // HEAD_DIM // GQA_KV_HEADS
    n_lat = jnp.where(i == 0, 0, n_chunks)
    for g in range(GQA_KV_HEADS):
        kcols = slice(128 * g, 128 * (g + 1))
        for hh in range(group):
            pair = q_ref[:, 128 * (g * group // 2 + hh // 2):128 * (g * group // 2 + hh // 2 + 1)]
            qs_ref[hh * tq:(hh + 1) * tq, :] = pair * hmask[hh % 2]
        qs = qs_ref[...]
        s = _dot_nt(qs, k_ref[0:n_ctx, kcols])
        m = jnp.max(s, axis=-1, keepdims=True)
        p = jnp.exp(s - m)
        m_ref[...] = m
        l_ref[...] = jnp.sum(p, axis=-1, keepdims=True)
        acc_ref[...] = _dot(p.astype(BF16), v_ref[0:n_ctx, kcols])

        def body(c, carry):
            off = pl.multiple_of(n_ctx + c * tk, tk // 2)
            s = _dot_nt(qs_ref[...], k_ref[pl.ds(off, tk), kcols])
            m_old = m_ref[...]
            m_new = jnp.maximum(m_old, jnp.max(s, axis=-1, keepdims=True))
            alpha = jnp.exp(m_old - m_new)
            p = jnp.exp(s - m_new)
            m_ref[...] = m_new
            l_ref[...] = alpha * l_ref[...] + jnp.sum(p, axis=-1, keepdims=True)
            acc_ref[...] = alpha * acc_ref[...] + _dot(p.astype(BF16), v_ref[pl.ds(off, tk), kcols])
            return carry

        lax.fori_loop(0, n_lat, body, 0)
        o = acc_ref[...] * (1.0 / l_ref[...])
        for pr in range(group // 2):
            col = 128 * (g * group // 2 + pr)
            o_ref[:, col:col + 128] = jnp.where(first_half, o[2 * pr * tq:(2 * pr + 1) * tq],
                                                 o[(2 * pr + 1) * tq:(2 * pr + 2) * tq]).astype(o_ref.dtype)


def _gqa(qn, kdup, vdup, n_ctx):
    t = qn.shape[0]
    tq = ROW_TILE
    assert n_ctx == tq
    tk = _pick(t - n_ctx, (512, 256))
    group = BRANCH_W // HEAD_DIM // GQA_KV_HEADS
    kvw2 = kdup.shape[1]
    return pl.pallas_call(
        functools.partial(_gqa_kernel, n_ctx=n_ctx, n_chunks=(t - n_ctx) // tk, tk=tk),
        grid=(t // tq,),
        in_specs=[pl.BlockSpec((tq, BRANCH_W), lambda i: (i, 0)),
                  pl.BlockSpec((t, kvw2), lambda i: (0, 0)),
                  pl.BlockSpec((t, kvw2), lambda i: (0, 0))],
        out_specs=pl.BlockSpec((tq, BRANCH_W), lambda i: (i, 0)),
        out_shape=jax.ShapeDtypeStruct((t, BRANCH_W), BF16),
        scratch_shapes=[pltpu.VMEM((group * tq, 128), BF16),
                        pltpu.VMEM((group * tq, 1), F32),
                        pltpu.VMEM((group * tq, 1), F32),
                        pltpu.VMEM((group * tq, 128), F32)],
        compiler_params=_params("parallel"),
        name="gqa",
    )(qn, kdup, vdup)


def _gelu_tanh(x):
    return 0.5 * x * (1.0 + jnp.tanh(np.sqrt(2.0 / np.pi).astype(np.float32) * (x + 0.044715 * (x * x * x))))


def _merge_kernel(a_ref, hf_ref, hb_ref, gr_ref, c_ref, d_ref, g0, g1, g2, g3, x_ref, mod_ref, wb_ref, wo_ref, o_ref):
    b = ((hf_ref[...] + hb_ref[...]) * _gelu_tanh(gr_ref[...].astype(F32))).astype(BF16)
    branches = (a_ref[...], b, c_ref[...], d_ref[...])
    gates = (g0, g1, g2, g3)
    y = None
    for n in range(N_BRANCH):
        term = _sigmoid(gates[n][...].astype(F32)) * _dot(branches[n], wb_ref[n])
        y = term if y is None else y + term
    out = _dot(y.astype(BF16), wo_ref[...])
    o_ref[...] = x_ref[...] + mod_ref[0, 0, 2:3, :] * out


def _merge(a, hf, hb, z, c, dd, x, mod, layer, wb, wo, n_ctx):
    t, d = x.shape
    tm = ROW_TILE
    nctx_tiles = n_ctx // tm
    row = lambda i: (i, 0)
    single = pl.Buffered(1)
    return pl.pallas_call(
        _merge_kernel,
        grid=(t // tm,),
        in_specs=[pl.BlockSpec((tm, BRANCH_W), row),
                  pl.BlockSpec((tm, BRANCH_W), row),
                  pl.BlockSpec((tm, BRANCH_W), row),
                  pl.BlockSpec((tm, BRANCH_W), lambda i: (i, COL_GR)),
                  pl.BlockSpec((tm, BRANCH_W), row),
                  pl.BlockSpec((tm, BRANCH_W), row)] + [
                      pl.BlockSpec((tm, d), functools.partial(lambda n, i: (i, n), n)) for n in range(N_BRANCH)] + [
                  pl.BlockSpec((tm, d), row),
                  pl.BlockSpec((1, 1, 8, d), lambda i: (layer, (i < nctx_tiles).astype(jnp.int32), 0, 0)),
                  pl.BlockSpec((N_BRANCH, BRANCH_W, d), lambda i: (0, 0, 0), pipeline_mode=single),
                  pl.BlockSpec((d, d), lambda i: (0, 0), pipeline_mode=single)],
        out_specs=pl.BlockSpec((tm, d), row),
        out_shape=jax.ShapeDtypeStruct((t, d), F32),
        compiler_params=_params("parallel"),
        name="merge",
    )(a, hf, hb, z, c, dd, z, z, z, z, x, mod, wb, wo)


def _ffn_kernel(gp, gm, gn, vp, vm, vn, dwg_ref, dwv_ref, wd_ref, x_ref, mod_ref, o_ref, acc_ref,
                *, tm, n_ctx, t_total):
    i = pl.program_id(0)
    k = pl.program_id(1)
    tk = gm.shape[1]
    row = lax.broadcasted_iota(jnp.int32, (tm, tk), 0)
    grow = row + i * tm
    no_prev = jnp.logical_or(grow == 0, grow == n_ctx)
    no_next = jnp.logical_or(grow == n_ctx - 1, grow == t_total - 1)

    def conv(p_ref, m_ref, n_ref, dw_ref):
        u = m_ref[...].astype(F32)
        prev = jnp.where(row == 0, p_ref[...].astype(F32)[HALO - 1:HALO, :], pltpu.roll(u, 1, 0))
        nxt = jnp.where(row == tm - 1, n_ref[...].astype(F32)[0:1, :], pltpu.roll(u, tm - 1, 0))
        prev = jnp.where(no_prev, 0.0, prev)
        nxt = jnp.where(no_next, 0.0, nxt)
        return dw_ref[0:1, :] * prev + dw_ref[1:2, :] * u + dw_ref[2:3, :] * nxt

    ug = conv(gp, gm, gn, dwg_ref)
    uv = conv(vp, vm, vn, dwv_ref)
    act = (ug * _sigmoid(ug) * uv).astype(BF16)
    part = _dot(act, wd_ref[...])

    @pl.when(k == 0)
    def _():
        acc_ref[...] = part

    @pl.when(k > 0)
    def _():
        acc_ref[...] = acc_ref[...] + part

    @pl.when(k == pl.num_programs(1) - 1)
    def _():
        rows = i * tm + lax.broadcasted_iota(jnp.int32, (tm, 1), 0)
        gate = jnp.where(rows < n_ctx, mod_ref[0, 1, 5:6, :], mod_ref[0, 0, 5:6, :])
        o_ref[...] = x_ref[...] + gate * acc_ref[...]


def _ffn_down(u, ffn_dw, wd, x, mod, layer, n_ctx):
    t, d = x.shape
    dff = wd.shape[0]
    tm = _pick(t, (768, 256))
    tk = _pick(dff, (512, 256))
    nk = dff // tk
    hpt = tm // HALO
    nh = t // HALO

    def specs(off):
        return [pl.BlockSpec((HALO, tk), lambda i, k: (jnp.maximum(i * hpt - 1, 0), k + off)),
                pl.BlockSpec((tm, tk), lambda i, k: (i, k + off)),
                pl.BlockSpec((HALO, tk), lambda i, k: (jnp.minimum((i + 1) * hpt, nh - 1), k + off))]

    return pl.pallas_call(
        functools.partial(_ffn_kernel, tm=tm, n_ctx=n_ctx, t_total=t),
        grid=(t // tm, nk),
        in_specs=specs(0) + specs(nk) + [
            pl.BlockSpec((3, tk), lambda i, k: (0, k)),
            pl.BlockSpec((3, tk), lambda i, k: (0, k + nk)),
            pl.BlockSpec((tk, d), lambda i, k: (k, 0)),
            pl.BlockSpec((tm, d), lambda i, k: (i, 0), pipeline_mode=pl.Buffered(1)),
            pl.BlockSpec((1, 2, 8, d), lambda i, k: (layer, 0, 0, 0))],
        out_specs=pl.BlockSpec((tm, d), lambda i, k: (i, 0)),
        out_shape=jax.ShapeDtypeStruct((t, d), F32),
        scratch_shapes=[pltpu.VMEM((tm, d), F32)],
        compiler_params=_params("parallel", "arbitrary"),
        name="ffn_down",
    )(u, u, u, u, u, u, ffn_dw, ffn_dw, wd, x, mod)


def _final_kernel(x_ref, g_ref, o_ref):
    x = x_ref[...]
    o_ref[...] = x * lax.rsqrt(jnp.mean(x * x, axis=-1, keepdims=True) + EPS) * g_ref[...]


def _final_norm(x, g, n_ctx):
    t, d = x.shape
    tm = ROW_TILE
    skip = n_ctx // tm
    return pl.pallas_call(
        _final_kernel,
        grid=((t - n_ctx) // tm,),
        in_specs=[pl.BlockSpec((tm, d), lambda i: (i + skip, 0)),
                  pl.BlockSpec((1, d), lambda i: (0, 0))],
        out_specs=pl.BlockSpec((tm, d), lambda i: (i, 0)),
        out_shape=jax.ShapeDtypeStruct((t - n_ctx, d), F32),
        compiler_params=_params("parallel"),
        name="final_norm",
    )(x, g.reshape(1, d))


def _block_diag(w):
    nb, bi, bj = w.shape
    eye = jnp.eye(nb, dtype=w.dtype)
    return jnp.einsum('nij,nm->nimj', w, eye).reshape(nb * bi, nb * bj)


def kernel(x, c, ctx, c_ctx, w_ada, b_ada, g_mix, w_in, na_rpb, rg_conv, w_rg, b_rg, rg_lambda, conf_dw, conf_ln_g,
           conf_ln_b, q_norm_g, k_norm_g, w_branch, w_out, g_ffn, w_up, ffn_dw, w_down, g_final):
    bsz, seq, d = x.shape
    n_ctx = ctx.shape[1]
    assert bsz == 1 and d == D_MODEL and n_ctx == ROW_TILE and seq % ROW_TILE == 0
    assert seq // GRID_W >= NA_KEY_ROWS

    cc = jnp.concatenate([c, c_ctx[None, :], jnp.zeros((6, d), F32)], axis=0)
    mod = _ada(cc, w_ada, b_ada)[:, :2, :].reshape(DEPTH, 2, N_MOD, d)
    mod = jnp.pad(mod, ((0, 0), (0, 0), (0, 8 - N_MOD), (0, 0)))

    xs = jnp.concatenate([ctx[0], x[0]], axis=0)
    cos, sin = _rope_tables(seq, n_ctx)

    for l in range(DEPTH):
        w_in_l = jnp.concatenate([w_in[l][:, N_REST:], w_in[l][:, :N_REST]], axis=1).astype(BF16)
        z = _proj(xs, g_mix[l], mod, l, 0, w_in_l, n_ctx, (1792, 896, 256))

        wbd = jnp.stack([jnp.concatenate([_block_diag(w_rg[l, dd, 0]), _block_diag(w_rg[l, dd, 1])], axis=1)
                         for dd in range(2)]).astype(BF16)
        brg = b_rg[l].reshape(2, 1, 2 * BRANCH_W)
        hf, hb = _rglru(z, rg_conv[l], wbd, brg, rg_lambda[l].reshape(2, 1, BRANCH_W))
        c_br = _conformer(z, conf_dw[l], conf_ln_g[l], conf_ln_b[l])
        a_br = _natten(z, _na_bias_table(na_rpb[l]))
        qn, kdup, vdup = _qkprep(z, cos, sin, q_norm_g[l], k_norm_g[l])
        d_br = _gqa(qn, kdup, vdup, n_ctx)
        xs = _merge(a_br, hf, hb, z, c_br, d_br, xs, mod, l, w_branch[l].astype(BF16), w_out[l].astype(BF16), n_ctx)

        u = _proj(xs, g_ffn[l], mod, l, 3, w_up[l].astype(BF16), n_ctx, (1024, 512, 256))
        xs = _ffn_down(u, ffn_dw[l], w_down[l].astype(BF16), xs, mod, l, n_ctx)

    return _final_norm(xs, g_final, n_ctx)[None]
```

```python
import functools

import numpy as np
import jax
import jax.numpy as jnp
from jax import lax
from jax.experimental import pallas as pl
from jax.experimental.pallas import tpu as pltpu

F32 = jnp.float32
BF16 = jnp.bfloat16

D_MODEL = 2048
DEPTH = 2
GRID_W = 64
HEAD_DIM = 64
N_BRANCH = 4
BRANCH_W = D_MODEL // N_BRANCH
WIN_ROWS = 8
WIN_COLS = 16
RG_C = 8.0
RG_CONV = 4
CONF_WIDTH = 31
GQA_KV_HEADS = 2
ROPE_THETA = 10000.0
D_FF = 5632
EPS = 1e-6
N_MOD = 6
N_GATE = N_BRANCH * D_MODEL
N_REST = 3 * BRANCH_W + 2 * BRANCH_W + 2 * BRANCH_W + BRANCH_W + 2 * GQA_KV_HEADS * HEAD_DIM
N_IN = N_GATE + N_REST

ROW_TILE = 256
HALO = 16
NA_ROWS = ROW_TILE // GRID_W
NA_KEY_ROWS = 3 * NA_ROWS
NEG = -1e30
VMEM_LIMIT = 56 * 1024 * 1024

COL_QA, COL_KA, COL_VA, COL_XR, COL_GR, COL_CV, COL_CG, COL_QD = (N_GATE // BRANCH_W + n for n in range(8))
COL_KD = (N_GATE + 8 * BRANCH_W) // 128
COL_VD = COL_KD + 1


def _params(*sem):
    return pltpu.CompilerParams(dimension_semantics=sem, vmem_limit_bytes=VMEM_LIMIT)


def _pick(total, candidates):
    for cand in candidates:
        if total % cand == 0:
            return cand
    raise ValueError(f"no tile for {total} in {candidates}")


def _dot(a, b):
    return jnp.dot(a, b, preferred_element_type=F32)


def _dot_nt(a, b):
    return lax.dot_general(a, b, (((1,), (1,)), ((), ())), preferred_element_type=F32)


def _sigmoid(x):
    return jax.nn.sigmoid(x)


def _ada_kernel(cc_ref, w_ref, b_ref, o_ref):
    a = cc_ref[...]
    a = a * _sigmoid(a)
    o_ref[0] = _dot(a.astype(BF16), w_ref[0].astype(BF16)) + b_ref[0]


def _ada(cc, w_ada, b_ada):
    depth, d, n = w_ada.shape
    tn = 1024
    return pl.pallas_call(
        _ada_kernel,
        grid=(depth, n // tn),
        in_specs=[pl.BlockSpec((8, d), lambda l, j: (0, 0)),
                  pl.BlockSpec((1, d, tn), lambda l, j: (l, 0, j)),
                  pl.BlockSpec((1, 1, tn), lambda l, j: (l, 0, j))],
        out_specs=pl.BlockSpec((1, 8, tn), lambda l, j: (l, 0, j)),
        out_shape=jax.ShapeDtypeStruct((depth, 8, n), F32),
        compiler_params=_params("parallel", "parallel"),
        name="ada",
    )(cc, w_ada, b_ada.reshape(depth, 1, n))


def _proj_kernel(x_ref, g_ref, mod_ref, w_ref, o_ref, h_ref, *, tm, n_ctx, shift_row):
    i = pl.program_id(0)

    @pl.when(pl.program_id(1) == 0)
    def _():
        x = x_ref[...]
        y = x * lax.rsqrt(jnp.mean(x * x, axis=-1, keepdims=True) + EPS) * g_ref[...]
        rows = i * tm + lax.broadcasted_iota(jnp.int32, (tm, 1), 0)
        is_ctx = rows < n_ctx
        shift = jnp.where(is_ctx, mod_ref[0, 1, shift_row:shift_row + 1, :], mod_ref[0, 0, shift_row:shift_row + 1, :])
        scale = jnp.where(is_ctx, mod_ref[0, 1, shift_row + 1:shift_row + 2, :],
                          mod_ref[0, 0, shift_row + 1:shift_row + 2, :])
        h_ref[...] = (y * (1.0 + scale) + shift).astype(BF16)

    o_ref[...] = _dot(h_ref[...], w_ref[...]).astype(o_ref.dtype)


def _proj(x, g, mod, layer, shift_row, w, n_ctx, tn_candidates):
    t, d = x.shape
    n = w.shape[1]
    tm = _pick(t, (768, 256))
    tn = _pick(n, tn_candidates)
    return pl.pallas_call(
        functools.partial(_proj_kernel, tm=tm, n_ctx=n_ctx, shift_row=shift_row),
        grid=(t // tm, n // tn),
        in_specs=[pl.BlockSpec((tm, d), lambda i, j: (i, 0)),
                  pl.BlockSpec((1, d), lambda i, j: (0, 0)),
                  pl.BlockSpec((1, 2, 8, d), lambda i, j: (layer, 0, 0, 0)),
                  pl.BlockSpec((d, tn), lambda i, j: (0, j))],
        out_specs=pl.BlockSpec((tm, tn), lambda i, j: (i, j)),
        out_shape=jax.ShapeDtypeStruct((t, n), BF16),
        scratch_shapes=[pltpu.VMEM((tm, d), BF16)],
        compiler_params=_params("parallel", "arbitrary"),
        name="proj",
    )(x, g.reshape(1, d), mod, w)


def _softplus(x):
    return jnp.maximum(x, 0.0) + jnp.log(1.0 + jnp.exp(-jnp.abs(x)))


def _scan_rows(a, b, reverse):
    n = a.shape[0]
    row = lax.broadcasted_iota(jnp.int32, a.shape, 0)
    d = 1
    while d < n:
        if reverse:
            keep = row < n - d
            shift = n - d
        else:
            keep = row >= d
            shift = d
        a_sh = jnp.where(keep, pltpu.roll(a, shift, 0), 1.0)
        b_sh = jnp.where(keep, pltpu.roll(b, shift, 0), 0.0)
        b = a * b_sh + b
        a = a * a_sh
        d *= 2
    return a, b


def _rg_kernel(xfp, xfm, xfn, xbp, xbm, xbn, cw_ref, wbd_ref, brg_ref, lam_ref, hf_ref, hb_ref,
               ext, carry_f, carry_b, *, nchunks):
    s = pl.program_id(0)
    tm = xfm.shape[0]
    bw = xfm.shape[1]

    @pl.when(s == 0)
    def _():
        carry_f[...] = jnp.zeros_like(carry_f)
        carry_b[...] = jnp.zeros_like(carry_b)

    cb = jnp.where(s == 0, 0, nchunks - s)

    def gates(d, xp, xm, xn, cidx):
        prev_ok = (cidx >= 2).astype(F32)
        next_ok = jnp.logical_and(cidx >= 1, cidx <= nchunks - 2).astype(F32)
        ext[0:HALO, :] = xp[...].astype(F32) * prev_ok
        ext[HALO:HALO + tm, :] = xm[...].astype(F32)
        ext[HALO + tm:2 * HALO + tm, :] = xn[...].astype(F32) * next_ok
        left = RG_CONV // 2
        xl = cw_ref[0:1, :] * ext[HALO - left:HALO - left + tm, :]
        for k in range(1, RG_CONV):
            xl = xl + cw_ref[k:k + 1, :] * ext[HALO - left + k:HALO - left + k + tm, :]
        g = _sigmoid(_dot(xl.astype(BF16), wbd_ref[d]) + brg_ref[d])
        log_a = (-RG_C * g[:, :bw]) * _softplus(-lam_ref[d])
        a = jnp.exp(log_a)
        b = jnp.sqrt(-jnp.tanh(log_a) * (a * a + 1.0)) * g[:, bw:] * xl
        return a, b

    a, b = gates(0, xfp, xfm, xfn, s)
    a, b = _scan_rows(a, b, reverse=False)
    h = b + a * carry_f[0:1, :]
    hf_ref[...] = h
    carry_f[...] = jnp.broadcast_to(h[tm - 1:tm, :], carry_f.shape)

    a, b = gates(1, xbp, xbm, xbn, cb)
    a, b = _scan_rows(a, b, reverse=True)
    h = b + a * carry_b[0:1, :]
    hb_ref[...] = h
    carry_b[...] = jnp.broadcast_to(h[0:1, :], carry_b.shape)


def _rglru(z, rg_conv, wbd, brg, lam):
    t = z.shape[0]
    tm = ROW_TILE
    nchunks = t // tm
    hpt = tm // HALO
    nh = t // HALO

    def bwd(s):
        return jnp.where(s == 0, 0, nchunks - s)

    def prev_map(cidx):
        return lambda s: (jnp.maximum(cidx(s) * hpt - 1, 0), COL_XR)

    def next_map(cidx):
        return lambda s: (jnp.minimum((cidx(s) + 1) * hpt, nh - 1), COL_XR)

    fwd = lambda s: s
    in_specs = [
        pl.BlockSpec((HALO, BRANCH_W), prev_map(fwd)),
        pl.BlockSpec((tm, BRANCH_W), lambda s: (s, COL_XR)),
        pl.BlockSpec((HALO, BRANCH_W), next_map(fwd)),
        pl.BlockSpec((HALO, BRANCH_W), prev_map(bwd)),
        pl.BlockSpec((tm, BRANCH_W), lambda s: (bwd(s), COL_XR)),
        pl.BlockSpec((HALO, BRANCH_W), next_map(bwd)),
        pl.BlockSpec((RG_CONV, BRANCH_W), lambda s: (0, 0)),
        pl.BlockSpec((2, BRANCH_W, 2 * BRANCH_W), lambda s: (0, 0, 0)),
        pl.BlockSpec((2, 1, 2 * BRANCH_W), lambda s: (0, 0, 0)),
        pl.BlockSpec((2, 1, BRANCH_W), lambda s: (0, 0, 0)),
    ]
    return pl.pallas_call(
        functools.partial(_rg_kernel, nchunks=nchunks),
        grid=(nchunks,),
        in_specs=in_specs,
        out_specs=[pl.BlockSpec((tm, BRANCH_W), lambda s: (s, 0)),
                   pl.BlockSpec((tm, BRANCH_W), lambda s: (bwd(s), 0))],
        out_shape=[jax.ShapeDtypeStruct((t, BRANCH_W), F32)] * 2,
        scratch_shapes=[pltpu.VMEM((tm + 2 * HALO, BRANCH_W), F32),
                        pltpu.VMEM((8, BRANCH_W), F32),
                        pltpu.VMEM((8, BRANCH_W), F32)],
        compiler_params=_params("arbitrary"),
        name="rglru",
    )(z, z, z, z, z, z, rg_conv, wbd, brg, lam)


def _conf_kernel(vp, vm, vn, gp, gm, gn, dw_ref, lng_ref, lnb_ref, o_ref, ext, *, nchunks):
    i = pl.program_id(0)
    tm = vm.shape[0]
    prev_ok = (i >= 2).astype(F32)
    next_ok = jnp.logical_and(i >= 1, i <= nchunks - 2).astype(F32)

    def glu(v, g):
        return v[...].astype(F32) * _sigmoid(g[...].astype(F32))

    ext[0:HALO, :] = glu(vp, gp) * prev_ok
    ext[HALO:HALO + tm, :] = glu(vm, gm)
    ext[HALO + tm:2 * HALO + tm, :] = glu(vn, gn) * next_ok
    base = HALO - CONF_WIDTH // 2
    acc = dw_ref[0:1, :] * ext[base:base + tm, :]
    for k in range(1, CONF_WIDTH):
        acc = acc + dw_ref[k:k + 1, :] * ext[base + k:base + k + tm, :]
    mu = jnp.mean(acc, axis=-1, keepdims=True)
    xc = acc - mu
    y = xc * lax.rsqrt(jnp.mean(xc * xc, axis=-1, keepdims=True) + EPS) * lng_ref[...] + lnb_ref[...]
    o_ref[...] = (y * _sigmoid(y)).astype(o_ref.dtype)


def _conformer(z, conf_dw, ln_g, ln_b):
    t = z.shape[0]
    tm = ROW_TILE
    nchunks = t // tm
    hpt = tm // HALO
    nh = t // HALO

    def specs(col):
        return [pl.BlockSpec((HALO, BRANCH_W), lambda i: (jnp.maximum(i * hpt - 1, 0), col)),
                pl.BlockSpec((tm, BRANCH_W), lambda i: (i, col)),
                pl.BlockSpec((HALO, BRANCH_W), lambda i: (jnp.minimum((i + 1) * hpt, nh - 1), col))]

    return pl.pallas_call(
        functools.partial(_conf_kernel, nchunks=nchunks),
        grid=(nchunks,),
        in_specs=specs(COL_CV) + specs(COL_CG) + [
            pl.BlockSpec((CONF_WIDTH, BRANCH_W), lambda i: (0, 0)),
            pl.BlockSpec((1, BRANCH_W), lambda i: (0, 0)),
            pl.BlockSpec((1, BRANCH_W), lambda i: (0, 0))],
        out_specs=pl.BlockSpec((tm, BRANCH_W), lambda i: (i, 0)),
        out_shape=jax.ShapeDtypeStruct((t, BRANCH_W), BF16),
        scratch_shapes=[pltpu.VMEM((tm + 2 * HALO, BRANCH_W), F32)],
        compiler_params=_params("parallel"),
        name="conformer",
    )(z, z, z, z, z, z, conf_dw, ln_g.reshape(1, -1), ln_b.reshape(1, -1))


def _na_bias_table(rpb):
    i = np.arange(NA_ROWS)[:, None]
    j = np.arange(NA_KEY_ROWS)[None, :]
    variants = [(np.zeros_like(i), i), (i, NA_ROWS + i), (np.full_like(i, NA_ROWS), 2 * NA_ROWS + i)]
    c = np.arange(GRID_W)[:, None]
    kc = np.arange(GRID_W)[None, :]
    col_start = np.clip(c - WIN_COLS // 2, 0, GRID_W - WIN_COLS)
    col_ok = (kc >= col_start) & (kc < col_start + WIN_COLS)
    nh = rpb.shape[0]
    padded = jnp.pad(rpb.astype(F32), ((0, 0), (0, 0), (GRID_W, GRID_W)))
    off = GRID_W + WIN_COLS - 1
    toeplitz = jnp.stack([padded[:, :, off - cc:off - cc + GRID_W] for cc in range(GRID_W)], axis=2)
    toeplitz = jnp.where(col_ok[None, None], toeplitz, NEG)
    masked = jnp.full((nh, GRID_W, GRID_W), NEG, F32)
    tables = [jnp.full((nh, ROW_TILE, NA_KEY_ROWS * GRID_W), NEG, F32)]
    for start, qrow in variants:
        row_ok = (j >= start) & (j < start + WIN_ROWS)
        dr = j - qrow + WIN_ROWS - 1
        rows = [jnp.concatenate([toeplitz[:, dr[qi, kj]] if row_ok[qi, kj] else masked for kj in range(NA_KEY_ROWS)],
                                axis=-1) for qi in range(NA_ROWS)]
        tables.append(jnp.stack(rows, axis=1).reshape(nh, ROW_TILE, NA_KEY_ROWS * GRID_W))
    return jnp.stack(tables)


def _na_kernel(q_ref, k0, k1, k2, kc, v0, v1, v2, vc, bias_ref, o_ref):
    tm = q_ref.shape[0]
    lane = lax.broadcasted_iota(jnp.int32, (1, 128), 1)
    scale = HEAD_DIM ** -0.5
    qmask = [jnp.where(lane < HEAD_DIM, scale, 0.0).astype(BF16), jnp.where(lane >= HEAD_DIM, scale, 0.0).astype(BF16)]
    first_half = lax.broadcasted_iota(jnp.int32, (tm, 128), 1) < HEAD_DIM
    for p in range(BRANCH_W // 128):
        cols = slice(128 * p, 128 * (p + 1))
        q = q_ref[:, cols]
        ks = [r[:, cols] for r in (k0, k1, k2, kc)]
        vs = [r[:, cols] for r in (v0, v1, v2, vc)]
        outs = []
        for e in range(2):
            qm = q * qmask[e]
            s = [_dot_nt(qm, k) for k in ks]
            for jb in range(3):
                s[jb] = s[jb] + bias_ref[0, 2 * p + e, :, tm * jb:tm * (jb + 1)]
            m = jnp.max(jnp.maximum(jnp.maximum(s[0], s[1]), jnp.maximum(s[2], s[3])), axis=-1, keepdims=True)
            ps = [jnp.exp(sj - m) for sj in s]
            l = jnp.sum(ps[0] + ps[1] + ps[2] + ps[3], axis=-1, keepdims=True)
            o = _dot(ps[0].astype(BF16), vs[0])
            for jb in range(1, 4):
                o = o + _dot(ps[jb].astype(BF16), vs[jb])
            outs.append(o * (1.0 / l))
        o_ref[:, cols] = jnp.where(first_half, outs[0], outs[1]).astype(o_ref.dtype)


def _natten(z, bias):
    t = z.shape[0]
    tm = ROW_TILE
    nt = t // tm
    assert nt >= 4

    def kbase(i):
        return jnp.clip(i - 1, 1, nt - 3)

    def variant(i):
        return jnp.where(i == 0, 0, jnp.where(i == 1, 1, jnp.where(i == nt - 1, 3, 2)))

    def kv_specs(col):
        return [pl.BlockSpec((tm, BRANCH_W), lambda i: (kbase(i), col)),
                pl.BlockSpec((tm, BRANCH_W), lambda i: (kbase(i) + 1, col)),
                pl.BlockSpec((tm, BRANCH_W), lambda i: (kbase(i) + 2, col)),
                pl.BlockSpec((tm, BRANCH_W), lambda i: (0, col))]

    nh = bias.shape[1]
    return pl.pallas_call(
        _na_kernel,
        grid=(nt,),
        in_specs=[pl.BlockSpec((tm, BRANCH_W), lambda i: (i, COL_QA))] + kv_specs(COL_KA) + kv_specs(COL_VA) + [
            pl.BlockSpec((1, nh, tm, 3 * tm), lambda i: (variant(i), 0, 0, 0))],
        out_specs=pl.BlockSpec((tm, BRANCH_W), lambda i: (i, 0)),
        out_shape=jax.ShapeDtypeStruct((t, BRANCH_W), BF16),
        compiler_params=_params("parallel"),
        name="natten",
    )(z, z, z, z, z, z, z, z, z, bias)


def _rope_tables(seq, n_ctx):
    half = HEAD_DIM // 2
    n_f = half // 2
    inv_freq = ROPE_THETA ** (-jnp.arange(n_f, dtype=F32) / n_f)
    pos = jnp.arange(seq)
    ang_r = (pos // GRID_W).astype(F32)[:, None] * inv_freq[None, :]
    ang_c = (pos % GRID_W).astype(F32)[:, None] * inv_freq[None, :]
    cos = jnp.concatenate([jnp.cos(ang_r), jnp.cos(ang_r), jnp.cos(ang_c), jnp.cos(ang_c)], axis=-1)
    sin = jnp.concatenate([-jnp.sin(ang_r), jnp.sin(ang_r), -jnp.sin(ang_c), jnp.sin(ang_c)], axis=-1)
    cos = jnp.concatenate([jnp.ones((n_ctx, HEAD_DIM), F32), cos], axis=0)
    sin = jnp.concatenate([jnp.zeros((n_ctx, HEAD_DIM), F32), sin], axis=0)
    return jnp.tile(cos, (1, 2)), jnp.tile(sin, (1, 2))


def _qkprep_kernel(q_ref, k_ref, v_ref, cos_ref, sin_ref, qg_ref, kg_ref, aq_ref, ak_ref, qo_ref, ko_ref, vo_ref):
    n_f = HEAD_DIM // 4

    def norm_rope(x, g, avg, cos, sin):
        n = x.shape[1]
        x2 = x * x
        hi = x2.astype(BF16)
        lo = (x2 - hi.astype(F32)).astype(BF16)
        ms = _dot(hi, avg) + _dot(lo, avg)
        y = x * lax.rsqrt(ms + EPS) * g
        lane = lax.broadcasted_iota(jnp.int32, x.shape, 1) % (2 * n_f)
        partner = jnp.where(lane < n_f, pltpu.roll(y, n - n_f, 1), pltpu.roll(y, n_f, 1))
        return y * cos + partner * sin

    cos = cos_ref[...]
    sin = sin_ref[...]
    reps = q_ref.shape[1] // 128
    q = norm_rope(q_ref[...].astype(F32), qg_ref[...], aq_ref[...],
                  jnp.concatenate([cos] * reps, axis=1), jnp.concatenate([sin] * reps, axis=1))
    qo_ref[...] = (q * HEAD_DIM ** -0.5).astype(BF16)

    first = lax.broadcasted_iota(jnp.int32, k_ref.shape, 1) < HEAD_DIM
    k = norm_rope(k_ref[...].astype(F32), kg_ref[...], ak_ref[...], cos, sin)
    k_sw = pltpu.roll(k, HEAD_DIM, 1)
    kdup = jnp.concatenate([jnp.where(first, k, k_sw), jnp.where(first, k_sw, k)], axis=1)
    ko_ref[...] = kdup.T.astype(BF16)
    v = v_ref[...].astype(F32)
    v_sw = pltpu.roll(v, HEAD_DIM, 1)
    vo_ref[...] = jnp.concatenate([jnp.where(first, v, 1.0), jnp.where(first, 1.0, v_sw),
                                   jnp.where(first, v_sw, 1.0), jnp.where(first, 1.0, v)], axis=1).astype(BF16)


def _head_avg(n):
    idx = np.arange(n) // HEAD_DIM
    return jnp.asarray((idx[:, None] == idx[None, :]).astype(np.float32) / HEAD_DIM, BF16)


def _qkprep(z, cos, sin, q_norm_g, k_norm_g):
    t = z.shape[0]
    tm = ROW_TILE
    kvw = GQA_KV_HEADS * HEAD_DIM
    qg = jnp.tile(q_norm_g.astype(F32), BRANCH_W // HEAD_DIM).reshape(1, BRANCH_W)
    kg = jnp.tile(k_norm_g.astype(F32), GQA_KV_HEADS).reshape(1, kvw)
    const = lambda i: (0, 0)
    return pl.pallas_call(
        _qkprep_kernel,
        grid=(t // tm,),
        in_specs=[pl.BlockSpec((tm, BRANCH_W), lambda i: (i, COL_QD)),
                  pl.BlockSpec((tm, kvw), lambda i: (i, COL_KD)),
                  pl.BlockSpec((tm, kvw), lambda i: (i, COL_VD)),
                  pl.BlockSpec((tm, kvw), lambda i: (i, 0)),
                  pl.BlockSpec((tm, kvw), lambda i: (i, 0)),
                  pl.BlockSpec((1, BRANCH_W), const),
                  pl.BlockSpec((1, kvw), const),
                  pl.BlockSpec((BRANCH_W, BRANCH_W), const),
                  pl.BlockSpec((kvw, kvw), const)],
        out_specs=[pl.BlockSpec((tm, BRANCH_W), lambda i: (i, 0)),
                   pl.BlockSpec((2 * kvw, tm), lambda i: (0, i)),
                   pl.BlockSpec((tm, 4 * kvw), lambda i: (i, 0))],
        out_shape=[jax.ShapeDtypeStruct((t, BRANCH_W), BF16),
                   jax.ShapeDtypeStruct((2 * kvw, t), BF16),
                   jax.ShapeDtypeStruct((t, 4 * kvw), BF16)],
        compiler_params=_params("parallel"),
        name="qkprep",
    )(z, z, z, cos, sin, qg, kg, _head_avg(BRANCH_W), _head_avg(kvw))


def _gqa_kernel(q_ref, kt_ref, va_ref, o_ref, qs_ref, s_ref, m_ref, acc_ref, *, n_ctx, n_chunks, tk):
    i = pl.program_id(0)
    tq = q_ref.shape[0]
    lane = lax.broadcasted_iota(jnp.int32, (1, 128), 1)
    hmask = [(lane < HEAD_DIM).astype(BF16), (lane >= HEAD_DIM).astype(BF16)]
    first_half = lax.broadcasted_iota(jnp.int32, (tq, 128), 1) < HEAD_DIM
    group = BRANCH_W // HEAD_DIM // GQA_KV_HEADS
    n_pairs = jnp.where(i == 0, 0, n_chunks // 2)
    for g in range(GQA_KV_HEADS):
        for hh in range(group):
            pair = q_ref[:, 128 * (g * group // 2 + hh // 2):128 * (g * group // 2 + hh // 2 + 1)]
            qs_ref[hh] = pair * hmask[hh % 2]

        def scores(off, size, slot):
            for hh in range(group):
                s_ref[slot, hh, :, 0:size] = _dot(qs_ref[hh], kt_ref[128 * g:128 * (g + 1), pl.ds(off, size)])

        def update(off, size, slot, first):
            for hh in range(group):
                s = s_ref[slot, hh, :, 0:size]
                mx = s[:, 0:128]
                for cc in range(1, size // 128):
                    mx = jnp.maximum(mx, s[:, 128 * cc:128 * (cc + 1)])
                rmax = jnp.max(mx, axis=-1, keepdims=True)
                vcol = 256 * g + 128 * (hh % 2)
                va = va_ref[pl.ds(off, size), vcol:vcol + 128]
                if first:
                    m_new = jnp.broadcast_to(rmax, (tq, 128))
                    p = jnp.exp(s - jnp.concatenate([m_new] * (size // 128), axis=1))
                    acc_ref[hh] = _dot(p.astype(BF16), va)
                else:
                    m_old = m_ref[hh]
                    m_new = jnp.maximum(m_old, rmax)
                    p = jnp.exp(s - jnp.concatenate([m_new] * (size // 128), axis=1))
                    acc_ref[hh] = jnp.exp(m_old - m_new) * acc_ref[hh] + _dot(p.astype(BF16), va)
                m_ref[hh] = m_new

        scores(0, n_ctx, 0)
        update(0, n_ctx, 0, True)
        scores(n_ctx, tk, 0)

        def body(c2, carry):
            off = pl.multiple_of(n_ctx + 2 * c2 * tk, 256)
            scores(off + tk, tk, 1)
            update(off, tk, 0, False)
            nxt = jnp.minimum(2 * c2 + 2, n_chunks - 1)
            scores(pl.multiple_of(n_ctx + nxt * tk, 256), tk, 0)
            update(off + tk, tk, 1, False)
            return carry

        lax.fori_loop(0, n_pairs, body, 0)
        for pr in range(group // 2):
            o0 = acc_ref[2 * pr]
            o1 = acc_ref[2 * pr + 1]
            num = jnp.where(first_half, o0, o1)
            den = jnp.where(first_half, pltpu.roll(o0, HEAD_DIM, 1), pltpu.roll(o1, HEAD_DIM, 1))
            col = 128 * (g * group // 2 + pr)
            o_ref[:, col:col + 128] = (num * (1.0 / den)).astype(o_ref.dtype)


def _gqa(qn, kt, va, n_ctx):
    t = qn.shape[0]
    tq = ROW_TILE
    assert n_ctx == tq
    tk = _pick(t - n_ctx, (2048, 1024, 512)) // 2
    n_chunks = (t - n_ctx) // tk
    assert tk >= n_ctx
    group = BRANCH_W // HEAD_DIM // GQA_KV_HEADS
    return pl.pallas_call(
        functools.partial(_gqa_kernel, n_ctx=n_ctx, n_chunks=n_chunks, tk=tk),
        grid=(t // tq,),
        in_specs=[pl.BlockSpec((tq, BRANCH_W), lambda i: (i, 0)),
                  pl.BlockSpec(kt.shape, lambda i: (0, 0), pipeline_mode=pl.Buffered(1)),
                  pl.BlockSpec(va.shape, lambda i: (0, 0), pipeline_mode=pl.Buffered(1))],
        out_specs=pl.BlockSpec((tq, BRANCH_W), lambda i: (i, 0)),
        out_shape=jax.ShapeDtypeStruct((t, BRANCH_W), BF16),
        scratch_shapes=[pltpu.VMEM((group, tq, 128), BF16),
                        pltpu.VMEM((2, group, tq, tk), F32),
                        pltpu.VMEM((group, tq, 128), F32),
                        pltpu.VMEM((group, tq, 128), F32)],
        compiler_params=_params("parallel"),
        name="gqa",
    )(qn, kt, va)


def _gelu_tanh(x):
    return 0.5 * x * (1.0 + jnp.tanh(np.sqrt(2.0 / np.pi).astype(np.float32) * (x + 0.044715 * (x * x * x))))


def _merge_kernel(a_ref, hf_ref, hb_ref, gr_ref, c_ref, d_ref, g0, g1, g2, g3, x_ref, mod_ref, wb_ref, wo_ref, o_ref):
    b = ((hf_ref[...] + hb_ref[...]) * _gelu_tanh(gr_ref[...].astype(F32))).astype(BF16)
    branches = (a_ref[...], b, c_ref[...], d_ref[...])
    gates = (g0, g1, g2, g3)
    y = None
    for n in range(N_BRANCH):
        term = _sigmoid(gates[n][...].astype(F32)) * _dot(branches[n], wb_ref[n])
        y = term if y is None else y + term
    out = _dot(y.astype(BF16), wo_ref[...])
    o_ref[...] = x_ref[...] + mod_ref[0, 0, 2:3, :] * out


def _merge(a, hf, hb, z, c, dd, x, mod, layer, wb, wo, n_ctx):
    t, d = x.shape
    tm = ROW_TILE
    nctx_tiles = n_ctx // tm
    row = lambda i: (i, 0)
    single = pl.Buffered(1)
    return pl.pallas_call(
        _merge_kernel,
        grid=(t // tm,),
        in_specs=[pl.BlockSpec((tm, BRANCH_W), row),
                  pl.BlockSpec((tm, BRANCH_W), row),
                  pl.BlockSpec((tm, BRANCH_W), row),
                  pl.BlockSpec((tm, BRANCH_W), lambda i: (i, COL_GR)),
                  pl.BlockSpec((tm, BRANCH_W), row),
                  pl.BlockSpec((tm, BRANCH_W), row)] + [
                      pl.BlockSpec((tm, d), functools.partial(lambda n, i: (i, n), n)) for n in range(N_BRANCH)] + [
                  pl.BlockSpec((tm, d), row),
                  pl.BlockSpec((1, 1, 8, d), lambda i: (layer, (i < nctx_tiles).astype(jnp.int32), 0, 0)),
                  pl.BlockSpec((N_BRANCH, BRANCH_W, d), lambda i: (0, 0, 0), pipeline_mode=single),
                  pl.BlockSpec((d, d), lambda i: (0, 0), pipeline_mode=single)],
        out_specs=pl.BlockSpec((tm, d), row),
        out_shape=jax.ShapeDtypeStruct((t, d), F32),
        compiler_params=_params("parallel"),
        name="merge",
    )(a, hf, hb, z, c, dd, z, z, z, z, x, mod, wb, wo)


def _ffn_up_kernel(xp_ref, x_ref, xn_ref, g_ref, mod_ref, wg_ref, wv_ref, dwg_ref, dwv_ref, o_ref,
                   h_ref, ug_ref, uv_ref, *, tm, n_ctx):
    i = pl.program_id(0)

    @pl.when(pl.program_id(1) == 0)
    def _():
        def norm_mod(x, first_row):
            y = x * lax.rsqrt(jnp.mean(x * x, axis=-1, keepdims=True) + EPS) * g_ref[...]
            rows = first_row + lax.broadcasted_iota(jnp.int32, (x.shape[0], 1), 0)
            is_ctx = rows < n_ctx
            shift = jnp.where(is_ctx, mod_ref[0, 1, 3:4, :], mod_ref[0, 0, 3:4, :])
            scale = jnp.where(is_ctx, mod_ref[0, 1, 4:5, :], mod_ref[0, 0, 4:5, :])
            return y * (1.0 + scale) + shift

        prev_ok = (i > 0).astype(F32)
        next_ok = (i < pl.num_programs(0) - 1).astype(F32)
        h_ref[0:HALO, :] = (norm_mod(xp_ref[...], i * tm - HALO) * prev_ok).astype(BF16)
        h_ref[HALO:HALO + tm, :] = norm_mod(x_ref[...], i * tm).astype(BF16)
        h_ref[HALO + tm:2 * HALO + tm, :] = (norm_mod(xn_ref[...], (i + 1) * tm) * next_ok).astype(BF16)

    rows = i * tm + lax.broadcasted_iota(jnp.int32, (tm, 1), 0)
    keep_prev = (rows != n_ctx).astype(F32)
    keep_next = (rows != n_ctx - 1).astype(F32)

    def conv(u_ref, w_ref, dw_ref):
        u_ref[...] = _dot(h_ref[...], w_ref[...])
        prev = u_ref[HALO - 1:HALO - 1 + tm, :] * keep_prev
        nxt = u_ref[HALO + 1:HALO + 1 + tm, :] * keep_next
        return dw_ref[0:1, :] * prev + dw_ref[1:2, :] * u_ref[HALO:HALO + tm, :] + dw_ref[2:3, :] * nxt

    ug = conv(ug_ref, wg_ref, dwg_ref)
    uv = conv(uv_ref, wv_ref, dwv_ref)
    o_ref[...] = (ug * _sigmoid(ug) * uv).astype(o_ref.dtype)


def _ffn_up(x, g, mod, layer, w_up, ffn_dw, n_ctx):
    t, d = x.shape
    dff = w_up.shape[1] // 2
    tm = _pick(t, (768, 256))
    tn = _pick(dff, (512, 256))
    nj = dff // tn
    hpt = tm // HALO
    nh = t // HALO
    return pl.pallas_call(
        functools.partial(_ffn_up_kernel, tm=tm, n_ctx=n_ctx),
        grid=(t // tm, nj),
        in_specs=[pl.BlockSpec((HALO, d), lambda i, j: (jnp.maximum(i * hpt - 1, 0), 0)),
                  pl.BlockSpec((tm, d), lambda i, j: (i, 0)),
                  pl.BlockSpec((HALO, d), lambda i, j: (jnp.minimum((i + 1) * hpt, nh - 1), 0)),
                  pl.BlockSpec((1, d), lambda i, j: (0, 0)),
                  pl.BlockSpec((1, 2, 8, d), lambda i, j: (layer, 0, 0, 0)),
                  pl.BlockSpec((d, tn), lambda i, j: (0, j)),
                  pl.BlockSpec((d, tn), lambda i, j: (0, j + nj)),
                  pl.BlockSpec((3, tn), lambda i, j: (0, j)),
                  pl.BlockSpec((3, tn), lambda i, j: (0, j + nj))],
        out_specs=pl.BlockSpec((tm, tn), lambda i, j: (i, j)),
        out_shape=jax.ShapeDtypeStruct((t, dff), BF16),
        scratch_shapes=[pltpu.VMEM((tm + 2 * HALO, d), BF16),
                        pltpu.VMEM((tm + 2 * HALO, tn), F32),
                        pltpu.VMEM((tm + 2 * HALO, tn), F32)],
        compiler_params=_params("parallel", "arbitrary"),
        name="ffn_up",
    )(x, x, x, g.reshape(1, d), mod, w_up, w_up, ffn_dw, ffn_dw)


def _ffn_down_kernel(a_ref, w_ref, x_ref, mod_ref, o_ref, *, tm, n_ctx):
    rows = pl.program_id(0) * tm + lax.broadcasted_iota(jnp.int32, (tm, 1), 0)
    gate = jnp.where(rows < n_ctx, mod_ref[0, 1, 5:6, :], mod_ref[0, 0, 5:6, :])
    o_ref[...] = x_ref[...] + gate * _dot(a_ref[...], w_ref[...])


def _ffn_down(act, wd, x, mod, layer, n_ctx):
    t, d = x.shape
    dff = wd.shape[0]
    tm = _pick(t, (768, 256))
    tn = 512
    return pl.pallas_call(
        functools.partial(_ffn_down_kernel, tm=tm, n_ctx=n_ctx),
        grid=(t // tm, d // tn),
        in_specs=[pl.BlockSpec((tm, dff), lambda i, j: (i, 0)),
                  pl.BlockSpec((dff, tn), lambda i, j: (0, j)),
                  pl.BlockSpec((tm, tn), lambda i, j: (i, j)),
                  pl.BlockSpec((1, 2, 8, tn), lambda i, j: (layer, 0, 0, j))],
        out_specs=pl.BlockSpec((tm, tn), lambda i, j: (i, j)),
        out_shape=jax.ShapeDtypeStruct((t, d), F32),
        compiler_params=_params("parallel", "arbitrary"),
        name="ffn_down",
    )(act, wd, x, mod)


def _final_kernel(x_ref, g_ref, o_ref):
    x = x_ref[...]
    o_ref[...] = x * lax.rsqrt(jnp.mean(x * x, axis=-1, keepdims=True) + EPS) * g_ref[...]


def _final_norm(x, g, n_ctx):
    t, d = x.shape
    tm = ROW_TILE
    skip = n_ctx // tm
    return pl.pallas_call(
        _final_kernel,
        grid=((t - n_ctx) // tm,),
        in_specs=[pl.BlockSpec((tm, d), lambda i: (i + skip, 0)),
                  pl.BlockSpec((1, d), lambda i: (0, 0))],
        out_specs=pl.BlockSpec((tm, d), lambda i: (i, 0)),
        out_shape=jax.ShapeDtypeStruct((t - n_ctx, d), F32),
        compiler_params=_params("parallel"),
        name="final_norm",
    )(x, g.reshape(1, d))


def _block_diag(w):
    nb, bi, bj = w.shape
    eye = jnp.eye(nb, dtype=w.dtype)
    return jnp.einsum('nij,nm->nimj', w, eye).reshape(nb * bi, nb * bj)


def kernel(x, c, ctx, c_ctx, w_ada, b_ada, g_mix, w_in, na_rpb, rg_conv, w_rg, b_rg, rg_lambda, conf_dw, conf_ln_g,
           conf_ln_b, q_norm_g, k_norm_g, w_branch, w_out, g_ffn, w_up, ffn_dw, w_down, g_final):
    bsz, seq, d = x.shape
    n_ctx = ctx.shape[1]
    assert bsz == 1 and d == D_MODEL and n_ctx == ROW_TILE and seq % ROW_TILE == 0
    assert seq // GRID_W >= NA_KEY_ROWS

    cc = jnp.concatenate([c, c_ctx[None, :], jnp.zeros((6, d), F32)], axis=0)
    mod = _ada(cc, w_ada, b_ada)[:, :2, :].reshape(DEPTH, 2, N_MOD, d)
    mod = jnp.pad(mod, ((0, 0), (0, 0), (0, 8 - N_MOD), (0, 0)))

    xs = jnp.concatenate([ctx[0], x[0]], axis=0)
    cos, sin = _rope_tables(seq, n_ctx)

    for l in range(DEPTH):
        w_in_l = jnp.concatenate([w_in[l][:, N_REST:], w_in[l][:, :N_REST]], axis=1).astype(BF16)
        z = _proj(xs, g_mix[l], mod, l, 0, w_in_l, n_ctx, (1792, 896, 256))

        wbd = jnp.stack([jnp.concatenate([_block_diag(w_rg[l, dd, 0]), _block_diag(w_rg[l, dd, 1])], axis=1)
                         for dd in range(2)]).astype(BF16)
        brg = b_rg[l].reshape(2, 1, 2 * BRANCH_W)
        hf, hb = _rglru(z, rg_conv[l], wbd, brg, rg_lambda[l].reshape(2, 1, BRANCH_W))
        c_br = _conformer(z, conf_dw[l], conf_ln_g[l], conf_ln_b[l])
        a_br = _natten(z, _na_bias_table(na_rpb[l]))
        qn, kt, va = _qkprep(z, cos, sin, q_norm_g[l], k_norm_g[l])
        d_br = _gqa(qn, kt, va, n_ctx)
        xs = _merge(a_br, hf, hb, z, c_br, d_br, xs, mod, l, w_branch[l].astype(BF16), w_out[l].astype(BF16), n_ctx)

        act = _ffn_up(xs, g_ffn[l], mod, l, w_up[l].astype(BF16), ffn_dw[l], n_ctx)
        xs = _ffn_down(act, w_down[l].astype(BF16), xs, mod, l, n_ctx)

    return _final_norm(xs, g_final, n_ctx)[None]
```

```python
import functools

import numpy as np
import jax
import jax.numpy as jnp
from jax import lax
from jax.experimental import pallas as pl
from jax.experimental.pallas import tpu as pltpu

F32 = jnp.float32
BF16 = jnp.bfloat16

D_MODEL = 2048
DEPTH = 2
GRID_W = 64
HEAD_DIM = 64
N_BRANCH = 4
BRANCH_W = D_MODEL // N_BRANCH
WIN_ROWS = 8
WIN_COLS = 16
RG_C = 8.0
RG_CONV = 4
CONF_WIDTH = 31
GQA_KV_HEADS = 2
ROPE_THETA = 10000.0
D_FF = 5632
EPS = 1e-6
N_MOD = 6
N_GATE = N_BRANCH * D_MODEL
N_REST = 3 * BRANCH_W + 2 * BRANCH_W + 2 * BRANCH_W + BRANCH_W + 2 * GQA_KV_HEADS * HEAD_DIM
N_IN = N_GATE + N_REST

ROW_TILE = 256
HALO = 16
NA_ROWS = ROW_TILE // GRID_W
NA_KEY_ROWS = 3 * NA_ROWS
NEG = -1e30
VMEM_LIMIT = 56 * 1024 * 1024

COL_QA, COL_KA, COL_VA, COL_XR, COL_GR, COL_CV, COL_CG, COL_QD = (N_GATE // BRANCH_W + n for n in range(8))
COL_KD = (N_GATE + 8 * BRANCH_W) // 128
COL_VD = COL_KD + 1


def _params(*sem):
    return pltpu.CompilerParams(dimension_semantics=sem, vmem_limit_bytes=VMEM_LIMIT)


def _pick(total, candidates):
    for cand in candidates:
        if total % cand == 0:
            return cand
    raise ValueError(f"no tile for {total} in {candidates}")


def _dot(a, b):
    return jnp.dot(a, b, preferred_element_type=F32)


def _dot_nt(a, b):
    return lax.dot_general(a, b, (((1,), (1,)), ((), ())), preferred_element_type=F32)


def _sigmoid(x):
    return jax.nn.sigmoid(x)


def _ada_kernel(cc_ref, w_ref, b_ref, o_ref):
    a = cc_ref[...]
    a = a * _sigmoid(a)
    o_ref[0] = _dot(a.astype(BF16), w_ref[0].astype(BF16)) + b_ref[0]


def _ada(cc, w_ada, b_ada):
    depth, d, n = w_ada.shape
    tn = 1024
    return pl.pallas_call(
        _ada_kernel,
        grid=(depth, n // tn),
        in_specs=[pl.BlockSpec((8, d), lambda l, j: (0, 0)),
                  pl.BlockSpec((1, d, tn), lambda l, j: (l, 0, j)),
                  pl.BlockSpec((1, 1, tn), lambda l, j: (l, 0, j))],
        out_specs=pl.BlockSpec((1, 8, tn), lambda l, j: (l, 0, j)),
        out_shape=jax.ShapeDtypeStruct((depth, 8, n), F32),
        compiler_params=_params("parallel", "parallel"),
        name="ada",
    )(cc, w_ada, b_ada.reshape(depth, 1, n))


def _proj_kernel(x_ref, g_ref, mod_ref, w_ref, o_ref, h_ref, *, tm, n_ctx, shift_row):
    i = pl.program_id(0)

    @pl.when(pl.program_id(1) == 0)
    def _():
        x = x_ref[...]
        y = x * lax.rsqrt(jnp.mean(x * x, axis=-1, keepdims=True) + EPS) * g_ref[...]
        rows = i * tm + lax.broadcasted_iota(jnp.int32, (tm, 1), 0)
        is_ctx = rows < n_ctx
        shift = jnp.where(is_ctx, mod_ref[0, 1, shift_row:shift_row + 1, :], mod_ref[0, 0, shift_row:shift_row + 1, :])
        scale = jnp.where(is_ctx, mod_ref[0, 1, shift_row + 1:shift_row + 2, :],
                          mod_ref[0, 0, shift_row + 1:shift_row + 2, :])
        h_ref[...] = (y * (1.0 + scale) + shift).astype(BF16)

    o_ref[...] = _dot(h_ref[...], w_ref[...]).astype(o_ref.dtype)


def _proj(x, g, mod, layer, shift_row, w, n_ctx, tn_candidates):
    t, d = x.shape
    n = w.shape[2]
    tm = _pick(t, (768, 256))
    tn = _pick(n, tn_candidates)
    return pl.pallas_call(
        functools.partial(_proj_kernel, tm=tm, n_ctx=n_ctx, shift_row=shift_row),
        grid=(t // tm, n // tn),
        in_specs=[pl.BlockSpec((tm, d), lambda i, j: (i, 0)),
                  pl.BlockSpec((1, d), lambda i, j: (0, 0)),
                  pl.BlockSpec((1, 2, 8, d), lambda i, j: (layer, 0, 0, 0)),
                  pl.BlockSpec((None, d, tn), lambda i, j: (layer, 0, j))],
        out_specs=pl.BlockSpec((tm, tn), lambda i, j: (i, j)),
        out_shape=jax.ShapeDtypeStruct((t, n), BF16),
        scratch_shapes=[pltpu.VMEM((tm, d), BF16)],
        compiler_params=_params("parallel", "arbitrary"),
        name="proj",
    )(x, g.reshape(1, d), mod, w)


def _softplus(x):
    return jnp.maximum(x, 0.0) + jnp.log(1.0 + jnp.exp(-jnp.abs(x)))


N_SEG = 8


def _scan_rows(a_ref, b_ref, h_ref, carry_ref, reverse):
    nblk, _, width = a_ref.shape
    seg_len = h_ref.shape[0] // N_SEG
    pitch = _seg_pitch(seg_len)
    for c in range(nblk):
        cols = slice(c * width, (c + 1) * width)
        h = jnp.zeros((N_SEG, width), F32)
        p = jnp.ones((N_SEG, width), F32)
        for j in (reversed(range(seg_len)) if reverse else range(seg_len)):
            rows = pl.ds(j, N_SEG, stride=pitch)
            aj = a_ref[c, rows, :]
            h = aj * h + b_ref[c, rows, :]
            p = aj * p
            b_ref[c, rows, :] = h
            a_ref[c, rows, :] = p
        state = carry_ref[0:1, cols]
        for k in (reversed(range(N_SEG)) if reverse else range(N_SEG)):
            h_ref[k * seg_len:(k + 1) * seg_len, cols] = (
                b_ref[c, k * pitch:k * pitch + seg_len, :] + a_ref[c, k * pitch:k * pitch + seg_len, :] * state)
            state = h[k:k + 1, :] + p[k:k + 1, :] * state
        carry_ref[:, cols] = jnp.broadcast_to(state, (carry_ref.shape[0], width))


def _seg_pitch(seg_len):
    return seg_len + 8 if (seg_len // 8) % 2 == 0 else seg_len


def _rg_kernel(xfp, xfm, xfn, xbp, xbm, xbn, cw_ref, wbd_ref, brg_ref, lam_ref, hf_ref, hb_ref,
               ext, a_ref, b_ref, carry_f, carry_b, *, nchunks):
    s = pl.program_id(0)
    tm = xfm.shape[0]
    bw = xfm.shape[1]

    @pl.when(s == 0)
    def _():
        carry_f[...] = jnp.zeros_like(carry_f)
        carry_b[...] = jnp.zeros_like(carry_b)

    cb = jnp.where(s == 0, 0, nchunks - s)

    def gates(d, xp, xm, xn, cidx):
        prev_ok = (cidx >= 2).astype(F32)
        next_ok = jnp.logical_and(cidx >= 1, cidx <= nchunks - 2).astype(F32)
        ext[0:HALO, :] = xp[...].astype(F32) * prev_ok
        ext[HALO:HALO + tm, :] = xm[...].astype(F32)
        ext[HALO + tm:2 * HALO + tm, :] = xn[...].astype(F32) * next_ok
        left = RG_CONV // 2
        xl = cw_ref[0:1, :] * ext[HALO - left:HALO - left + tm, :]
        for k in range(1, RG_CONV):
            xl = xl + cw_ref[k:k + 1, :] * ext[HALO - left + k:HALO - left + k + tm, :]
        g = _sigmoid(_dot(xl.astype(BF16), wbd_ref[d]) + brg_ref[d])
        log_a = (-RG_C * g[:, :bw]) * _softplus(-lam_ref[d])
        a = jnp.exp(log_a)
        b = jnp.sqrt(-jnp.tanh(log_a) * (a * a + 1.0)) * g[:, bw:] * xl
        seg_len = tm // N_SEG
        pitch = _seg_pitch(seg_len)
        for c in range(bw // 128):
            for k in range(N_SEG):
                a_ref[d, c, k * pitch:k * pitch + seg_len, :] = a[k * seg_len:(k + 1) * seg_len, 128 * c:128 * (c + 1)]
                b_ref[d, c, k * pitch:k * pitch + seg_len, :] = b[k * seg_len:(k + 1) * seg_len, 128 * c:128 * (c + 1)]

    gates(0, xfp, xfm, xfn, s)
    gates(1, xbp, xbm, xbn, cb)
    _scan_rows(a_ref.at[0], b_ref.at[0], hf_ref, carry_f, reverse=False)
    _scan_rows(a_ref.at[1], b_ref.at[1], hb_ref, carry_b, reverse=True)


def _rglru(z, rg_conv, wbd, brg, lam):
    t = z.shape[0]
    tm = ROW_TILE
    nchunks = t // tm
    hpt = tm // HALO
    nh = t // HALO

    def bwd(s):
        return jnp.where(s == 0, 0, nchunks - s)

    def prev_map(cidx):
        return lambda s: (jnp.maximum(cidx(s) * hpt - 1, 0), COL_XR)

    def next_map(cidx):
        return lambda s: (jnp.minimum((cidx(s) + 1) * hpt, nh - 1), COL_XR)

    fwd = lambda s: s
    in_specs = [
        pl.BlockSpec((HALO, BRANCH_W), prev_map(fwd)),
        pl.BlockSpec((tm, BRANCH_W), lambda s: (s, COL_XR)),
        pl.BlockSpec((HALO, BRANCH_W), next_map(fwd)),
        pl.BlockSpec((HALO, BRANCH_W), prev_map(bwd)),
        pl.BlockSpec((tm, BRANCH_W), lambda s: (bwd(s), COL_XR)),
        pl.BlockSpec((HALO, BRANCH_W), next_map(bwd)),
        pl.BlockSpec((RG_CONV, BRANCH_W), lambda s: (0, 0)),
        pl.BlockSpec((2, BRANCH_W, 2 * BRANCH_W), lambda s: (0, 0, 0)),
        pl.BlockSpec((2, 1, 2 * BRANCH_W), lambda s: (0, 0, 0)),
        pl.BlockSpec((2, 1, BRANCH_W), lambda s: (0, 0, 0)),
    ]
    return pl.pallas_call(
        functools.partial(_rg_kernel, nchunks=nchunks),
        grid=(nchunks,),
        in_specs=in_specs,
        out_specs=[pl.BlockSpec((tm, BRANCH_W), lambda s: (s, 0)),
                   pl.BlockSpec((tm, BRANCH_W), lambda s: (bwd(s), 0))],
        out_shape=[jax.ShapeDtypeStruct((t, BRANCH_W), F32)] * 2,
        scratch_shapes=[pltpu.VMEM((tm + 2 * HALO, BRANCH_W), F32),
                        pltpu.VMEM((2, BRANCH_W // 128, N_SEG * _seg_pitch(tm // N_SEG), 128), F32),
                        pltpu.VMEM((2, BRANCH_W // 128, N_SEG * _seg_pitch(tm // N_SEG), 128), F32),
                        pltpu.VMEM((8, BRANCH_W), F32),
                        pltpu.VMEM((8, BRANCH_W), F32)],
        compiler_params=_params("arbitrary"),
        name="rglru",
    )(z, z, z, z, z, z, rg_conv, wbd, brg, lam)


def _conf_kernel(vp, vm, vn, gp, gm, gn, dw_ref, lng_ref, lnb_ref, o_ref, ext, shifted, *, nchunks):
    i = pl.program_id(0)
    tm = vm.shape[0]
    prev_ok = (i >= 2).astype(F32)
    next_ok = jnp.logical_and(i >= 1, i <= nchunks - 2).astype(F32)

    def glu(v, g):
        return v[...].astype(F32) * _sigmoid(g[...].astype(F32))

    ext[0:HALO, :] = glu(vp, gp) * prev_ok
    ext[HALO:HALO + tm, :] = glu(vm, gm)
    ext[HALO + tm:2 * HALO + tm, :] = glu(vn, gn) * next_ok
    base = HALO - CONF_WIDTH // 2
    span = tm + 2 * HALO - 8
    for r in range(1, 8):
        shifted[r - 1] = ext[r:r + span, :]
    acc = None
    for k in range(CONF_WIDTH):
        q, r = divmod(base + k, 8)
        rows = ext[8 * q:8 * q + tm, :] if r == 0 else shifted[r - 1, 8 * q:8 * q + tm, :]
        term = dw_ref[k:k + 1, :] * rows
        acc = term if acc is None else acc + term
    mu = jnp.mean(acc, axis=-1, keepdims=True)
    xc = acc - mu
    y = xc * lax.rsqrt(jnp.mean(xc * xc, axis=-1, keepdims=True) + EPS) * lng_ref[...] + lnb_ref[...]
    o_ref[...] = (y * _sigmoid(y)).astype(o_ref.dtype)


def _conformer(z, conf_dw, ln_g, ln_b):
    t = z.shape[0]
    tm = ROW_TILE
    nchunks = t // tm
    hpt = tm // HALO
    nh = t // HALO

    def specs(col):
        return [pl.BlockSpec((HALO, BRANCH_W), lambda i: (jnp.maximum(i * hpt - 1, 0), col)),
                pl.BlockSpec((tm, BRANCH_W), lambda i: (i, col)),
                pl.BlockSpec((HALO, BRANCH_W), lambda i: (jnp.minimum((i + 1) * hpt, nh - 1), col))]

    return pl.pallas_call(
        functools.partial(_conf_kernel, nchunks=nchunks),
        grid=(nchunks,),
        in_specs=specs(COL_CV) + specs(COL_CG) + [
            pl.BlockSpec((CONF_WIDTH, BRANCH_W), lambda i: (0, 0)),
            pl.BlockSpec((1, BRANCH_W), lambda i: (0, 0)),
            pl.BlockSpec((1, BRANCH_W), lambda i: (0, 0))],
        out_specs=pl.BlockSpec((tm, BRANCH_W), lambda i: (i, 0)),
        out_shape=jax.ShapeDtypeStruct((t, BRANCH_W), BF16),
        scratch_shapes=[pltpu.VMEM((tm + 2 * HALO, BRANCH_W), F32),
                        pltpu.VMEM((7, tm + 2 * HALO - 8, BRANCH_W), F32)],
        compiler_params=_params("parallel"),
        name="conformer",
    )(z, z, z, z, z, z, conf_dw, ln_g.reshape(1, -1), ln_b.reshape(1, -1))


def _na_bias_table(rpb):
    i = np.arange(NA_ROWS)[:, None]
    j = np.arange(NA_KEY_ROWS)[None, :]
    variants = [(np.zeros_like(i), i), (i, NA_ROWS + i), (np.full_like(i, NA_ROWS), 2 * NA_ROWS + i)]
    c = np.arange(GRID_W)[:, None]
    kc = np.arange(GRID_W)[None, :]
    col_start = np.clip(c - WIN_COLS // 2, 0, GRID_W - WIN_COLS)
    col_ok = (kc >= col_start) & (kc < col_start + WIN_COLS)
    nh = rpb.shape[0]
    padded = jnp.pad(rpb.astype(F32), ((0, 0), (0, 0), (GRID_W, GRID_W)))
    off = GRID_W + WIN_COLS - 1
    toeplitz = jnp.stack([padded[:, :, off - cc:off - cc + GRID_W] for cc in range(GRID_W)], axis=2)
    toeplitz = jnp.where(col_ok[None, None], toeplitz, NEG)
    masked = jnp.full((nh, GRID_W, GRID_W), NEG, F32)
    tables = [jnp.full((nh, ROW_TILE, NA_KEY_ROWS * GRID_W), NEG, F32)]
    for start, qrow in variants:
        row_ok = (j >= start) & (j < start + WIN_ROWS)
        dr = j - qrow + WIN_ROWS - 1
        rows = [jnp.concatenate([toeplitz[:, dr[qi, kj]] if row_ok[qi, kj] else masked for kj in range(NA_KEY_ROWS)],
                                axis=-1) for qi in range(NA_ROWS)]
        tables.append(jnp.stack(rows, axis=1).reshape(nh, ROW_TILE, NA_KEY_ROWS * GRID_W))
    return jnp.stack(tables)


def _na_kernel(q_ref, k0, k1, k2, kc, v0, v1, v2, vc, bias_ref, o_ref):
    tm = q_ref.shape[0]
    lane = lax.broadcasted_iota(jnp.int32, (1, 128), 1)
    scale = HEAD_DIM ** -0.5
    qmask = [jnp.where(lane < HEAD_DIM, scale, 0.0).astype(BF16), jnp.where(lane >= HEAD_DIM, scale, 0.0).astype(BF16)]
    first_half = lax.broadcasted_iota(jnp.int32, (tm, 128), 1) < HEAD_DIM
    for p in range(BRANCH_W // 128):
        cols = slice(128 * p, 128 * (p + 1))
        q = q_ref[:, cols]
        ks = [r[:, cols] for r in (k0, k1, k2, kc)]
        vs = [r[:, cols] for r in (v0, v1, v2, vc)]
        outs = []
        for e in range(2):
            qm = q * qmask[e]
            s = [_dot_nt(qm, k) for k in ks]
            for jb in range(3):
                s[jb] = s[jb] + bias_ref[0, 2 * p + e, :, tm * jb:tm * (jb + 1)]
            m = jnp.max(jnp.maximum(jnp.maximum(s[0], s[1]), jnp.maximum(s[2], s[3])), axis=-1, keepdims=True)
            ps = [jnp.exp(sj - m) for sj in s]
            l = jnp.sum(ps[0] + ps[1] + ps[2] + ps[3], axis=-1, keepdims=True)
            o = _dot(ps[0].astype(BF16), vs[0])
            for jb in range(1, 4):
                o = o + _dot(ps[jb].astype(BF16), vs[jb])
            outs.append(o * (1.0 / l))
        o_ref[:, cols] = jnp.where(first_half, outs[0], outs[1]).astype(o_ref.dtype)


def _natten(z, bias):
    t = z.shape[0]
    tm = ROW_TILE
    nt = t // tm
    assert nt >= 4

    def kbase(i):
        return jnp.clip(i - 1, 1, nt - 3)

    def variant(i):
        return jnp.where(i == 0, 0, jnp.where(i == 1, 1, jnp.where(i == nt - 1, 3, 2)))

    def kv_specs(col):
        return [pl.BlockSpec((tm, BRANCH_W), lambda i: (kbase(i), col)),
                pl.BlockSpec((tm, BRANCH_W), lambda i: (kbase(i) + 1, col)),
                pl.BlockSpec((tm, BRANCH_W), lambda i: (kbase(i) + 2, col)),
                pl.BlockSpec((tm, BRANCH_W), lambda i: (0, col))]

    nh = bias.shape[1]
    return pl.pallas_call(
        _na_kernel,
        grid=(nt,),
        in_specs=[pl.BlockSpec((tm, BRANCH_W), lambda i: (i, COL_QA))] + kv_specs(COL_KA) + kv_specs(COL_VA) + [
            pl.BlockSpec((1, nh, tm, 3 * tm), lambda i: (variant(i), 0, 0, 0))],
        out_specs=pl.BlockSpec((tm, BRANCH_W), lambda i: (i, 0)),
        out_shape=jax.ShapeDtypeStruct((t, BRANCH_W), BF16),
        compiler_params=_params("parallel"),
        name="natten",
    )(z, z, z, z, z, z, z, z, z, bias)


def _rope_tables(seq, n_ctx):
    half = HEAD_DIM // 2
    n_f = half // 2
    inv_freq = ROPE_THETA ** (-jnp.arange(n_f, dtype=F32) / n_f)
    pos = jnp.arange(seq)
    ang_r = (pos // GRID_W).astype(F32)[:, None] * inv_freq[None, :]
    ang_c = (pos % GRID_W).astype(F32)[:, None] * inv_freq[None, :]
    cos = jnp.concatenate([jnp.cos(ang_r), jnp.cos(ang_r), jnp.cos(ang_c), jnp.cos(ang_c)], axis=-1)
    sin = jnp.concatenate([-jnp.sin(ang_r), jnp.sin(ang_r), -jnp.sin(ang_c), jnp.sin(ang_c)], axis=-1)
    cos = jnp.concatenate([jnp.ones((n_ctx, HEAD_DIM), F32), cos], axis=0)
    sin = jnp.concatenate([jnp.zeros((n_ctx, HEAD_DIM), F32), sin], axis=0)
    return jnp.tile(cos, (1, 2)), jnp.tile(sin, (1, 2))


def _qkprep_kernel(q_ref, k_ref, v_ref, cos_ref, sin_ref, qg_ref, kg_ref, aq_ref, ak_ref, qo_ref, ko_ref, vo_ref):
    n_f = HEAD_DIM // 4

    def norm_rope(x, g, avg, cos, sin):
        n = x.shape[1]
        x2 = x * x
        hi = x2.astype(BF16)
        lo = (x2 - hi.astype(F32)).astype(BF16)
        ms = _dot(hi, avg) + _dot(lo, avg)
        y = x * lax.rsqrt(ms + EPS) * g
        lane = lax.broadcasted_iota(jnp.int32, x.shape, 1) % (2 * n_f)
        partner = jnp.where(lane < n_f, pltpu.roll(y, n - n_f, 1), pltpu.roll(y, n_f, 1))
        return y * cos + partner * sin

    cos = cos_ref[...]
    sin = sin_ref[...]
    reps = q_ref.shape[1] // 128
    q = norm_rope(q_ref[...].astype(F32), qg_ref[...], aq_ref[...],
                  jnp.concatenate([cos] * reps, axis=1), jnp.concatenate([sin] * reps, axis=1))
    qo_ref[...] = (q * HEAD_DIM ** -0.5).astype(BF16)

    first = lax.broadcasted_iota(jnp.int32, k_ref.shape, 1) < HEAD_DIM
    k = norm_rope(k_ref[...].astype(F32), kg_ref[...], ak_ref[...], cos, sin)
    k_sw = pltpu.roll(k, HEAD_DIM, 1)
    kdup = jnp.concatenate([jnp.where(first, k, k_sw), jnp.where(first, k_sw, k)], axis=1)
    ko_ref[...] = kdup.T.astype(BF16)
    v = v_ref[...].astype(F32)
    v_sw = pltpu.roll(v, HEAD_DIM, 1)
    vo_ref[...] = jnp.concatenate([jnp.where(first, v, 1.0), jnp.where(first, 1.0, v_sw),
                                   jnp.where(first, v_sw, 1.0), jnp.where(first, 1.0, v)], axis=1).astype(BF16)


def _head_avg(n):
    idx = np.arange(n) // HEAD_DIM
    return jnp.asarray((idx[:, None] == idx[None, :]).astype(np.float32) / HEAD_DIM, BF16)


def _qkprep(z, cos, sin, q_norm_g, k_norm_g):
    t = z.shape[0]
    tm = ROW_TILE
    kvw = GQA_KV_HEADS * HEAD_DIM
    qg = jnp.tile(q_norm_g.astype(F32), BRANCH_W // HEAD_DIM).reshape(1, BRANCH_W)
    kg = jnp.tile(k_norm_g.astype(F32), GQA_KV_HEADS).reshape(1, kvw)
    const = lambda i: (0, 0)
    return pl.pallas_call(
        _qkprep_kernel,
        grid=(t // tm,),
        in_specs=[pl.BlockSpec((tm, BRANCH_W), lambda i: (i, COL_QD)),
                  pl.BlockSpec((tm, kvw), lambda i: (i, COL_KD)),
                  pl.BlockSpec((tm, kvw), lambda i: (i, COL_VD)),
                  pl.BlockSpec((tm, kvw), lambda i: (i, 0)),
                  pl.BlockSpec((tm, kvw), lambda i: (i, 0)),
                  pl.BlockSpec((1, BRANCH_W), const),
                  pl.BlockSpec((1, kvw), const),
                  pl.BlockSpec((BRANCH_W, BRANCH_W), const),
                  pl.BlockSpec((kvw, kvw), const)],
        out_specs=[pl.BlockSpec((tm, BRANCH_W), lambda i: (i, 0)),
                   pl.BlockSpec((2 * kvw, tm), lambda i: (0, i)),
                   pl.BlockSpec((tm, 4 * kvw), lambda i: (i, 0))],
        out_shape=[jax.ShapeDtypeStruct((t, BRANCH_W), BF16),
                   jax.ShapeDtypeStruct((2 * kvw, t), BF16),
                   jax.ShapeDtypeStruct((t, 4 * kvw), BF16)],
        compiler_params=_params("parallel"),
        name="qkprep",
    )(z, z, z, cos, sin, qg, kg, _head_avg(BRANCH_W), _head_avg(kvw))


def _gqa_kernel(q_ref, kt_ref, va_ref, o_ref, qs_ref, s_ref, m_ref, acc_ref, *, n_ctx, n_chunks, tk):
    i = pl.program_id(0)
    tq = q_ref.shape[0]
    lane = lax.broadcasted_iota(jnp.int32, (1, 128), 1)
    hmask = [(lane < HEAD_DIM).astype(BF16), (lane >= HEAD_DIM).astype(BF16)]
    first_half = lax.broadcasted_iota(jnp.int32, (tq, 128), 1) < HEAD_DIM
    group = BRANCH_W // HEAD_DIM // GQA_KV_HEADS
    n_pairs = jnp.where(i == 0, 0, n_chunks // 2)
    for g in range(GQA_KV_HEADS):
        for hh in range(group):
            pair = q_ref[:, 128 * (g * group // 2 + hh // 2):128 * (g * group // 2 + hh // 2 + 1)]
            qs_ref[hh] = pair * hmask[hh % 2]

        def scores(off, size, slot):
            for hh in range(group):
                s_ref[slot, hh, :, 0:size] = _dot(qs_ref[hh], kt_ref[128 * g:128 * (g + 1), pl.ds(off, size)])

        def update(off, size, slot, first):
            for hh in range(group):
                s = s_ref[slot, hh, :, 0:size]
                mx = s[:, 0:128]
                for cc in range(1, size // 128):
                    mx = jnp.maximum(mx, s[:, 128 * cc:128 * (cc + 1)])
                rmax = jnp.max(mx, axis=-1, keepdims=True)
                vcol = 256 * g + 128 * (hh % 2)
                va = va_ref[pl.ds(off, size), vcol:vcol + 128]
                if first:
                    m_new = jnp.broadcast_to(rmax, (tq, 128))
                    p = jnp.exp(s - jnp.concatenate([m_new] * (size // 128), axis=1))
                    acc_ref[hh] = _dot(p.astype(BF16), va)
                else:
                    m_old = m_ref[hh]
                    m_new = jnp.maximum(m_old, rmax)
                    p = jnp.exp(s - jnp.concatenate([m_new] * (size // 128), axis=1))
                    acc_ref[hh] = jnp.exp(m_old - m_new) * acc_ref[hh] + _dot(p.astype(BF16), va)
                m_ref[hh] = m_new

        scores(0, n_ctx, 1)
        scores(n_ctx, tk, 0)
        update(0, n_ctx, 1, True)

        def body(c2, carry):
            off = pl.multiple_of(n_ctx + 2 * c2 * tk, 256)
            scores(off + tk, tk, 1)
            update(off, tk, 0, False)
            nxt = jnp.minimum(2 * c2 + 2, n_chunks - 1)
            scores(pl.multiple_of(n_ctx + nxt * tk, 256), tk, 0)
            update(off + tk, tk, 1, False)
            return carry

        lax.fori_loop(0, n_pairs, body, 0)
        for pr in range(group // 2):
            o0 = acc_ref[2 * pr]
            o1 = acc_ref[2 * pr + 1]
            num = jnp.where(first_half, o0, o1)
            den = jnp.where(first_half, pltpu.roll(o0, HEAD_DIM, 1), pltpu.roll(o1, HEAD_DIM, 1))
            col = 128 * (g * group // 2 + pr)
            o_ref[:, col:col + 128] = (num * (1.0 / den)).astype(o_ref.dtype)


def _gqa(qn, kt, va, n_ctx):
    t = qn.shape[0]
    tq = ROW_TILE
    assert n_ctx == tq
    tk = _pick(t - n_ctx, (2048, 1024, 512)) // 2
    n_chunks = (t - n_ctx) // tk
    assert tk >= n_ctx
    group = BRANCH_W // HEAD_DIM // GQA_KV_HEADS
    return pl.pallas_call(
        functools.partial(_gqa_kernel, n_ctx=n_ctx, n_chunks=n_chunks, tk=tk),
        grid=(t // tq,),
        in_specs=[pl.BlockSpec((tq, BRANCH_W), lambda i: (i, 0)),
                  pl.BlockSpec(kt.shape, lambda i: (0, 0), pipeline_mode=pl.Buffered(1)),
                  pl.BlockSpec(va.shape, lambda i: (0, 0), pipeline_mode=pl.Buffered(1))],
        out_specs=pl.BlockSpec((tq, BRANCH_W), lambda i: (i, 0)),
        out_shape=jax.ShapeDtypeStruct((t, BRANCH_W), BF16),
        scratch_shapes=[pltpu.VMEM((group, tq, 128), BF16),
                        pltpu.VMEM((2, group, tq, tk), F32),
                        pltpu.VMEM((group, tq, 128), F32),
                        pltpu.VMEM((group, tq, 128), F32)],
        compiler_params=_params("parallel"),
        name="gqa",
    )(qn, kt, va)


def _gelu_tanh(x):
    return 0.5 * x * (1.0 + jnp.tanh(np.sqrt(2.0 / np.pi).astype(np.float32) * (x + 0.044715 * (x * x * x))))


def _merge_kernel(a_ref, hf_ref, hb_ref, gr_ref, c_ref, d_ref, g0, g1, g2, g3, x_ref, mod_ref, wb_ref, wo_ref, o_ref):
    b = ((hf_ref[...] + hb_ref[...]) * _gelu_tanh(gr_ref[...].astype(F32))).astype(BF16)
    branches = (a_ref[...], b, c_ref[...], d_ref[...])
    gates = (g0, g1, g2, g3)
    y = None
    for n in range(N_BRANCH):
        term = _sigmoid(gates[n][...].astype(F32)) * _dot(branches[n], wb_ref[n])
        y = term if y is None else y + term
    out = _dot(y.astype(BF16), wo_ref[...])
    o_ref[...] = x_ref[...] + mod_ref[0, 0, 2:3, :] * out


def _merge(a, hf, hb, z, c, dd, x, mod, layer, wb, wo, n_ctx):
    t, d = x.shape
    tm = ROW_TILE
    nctx_tiles = n_ctx // tm
    row = lambda i: (i, 0)
    single = pl.Buffered(1)
    return pl.pallas_call(
        _merge_kernel,
        grid=(t // tm,),
        in_specs=[pl.BlockSpec((tm, BRANCH_W), row),
                  pl.BlockSpec((tm, BRANCH_W), row),
                  pl.BlockSpec((tm, BRANCH_W), row),
                  pl.BlockSpec((tm, BRANCH_W), lambda i: (i, COL_GR)),
                  pl.BlockSpec((tm, BRANCH_W), row),
                  pl.BlockSpec((tm, BRANCH_W), row)] + [
                      pl.BlockSpec((tm, d), functools.partial(lambda n, i: (i, n), n)) for n in range(N_BRANCH)] + [
                  pl.BlockSpec((tm, d), row),
                  pl.BlockSpec((1, 1, 8, d), lambda i: (layer, (i < nctx_tiles).astype(jnp.int32), 0, 0)),
                  pl.BlockSpec((None, N_BRANCH, BRANCH_W, d), lambda i: (layer, 0, 0, 0), pipeline_mode=single),
                  pl.BlockSpec((None, d, d), lambda i: (layer, 0, 0), pipeline_mode=single)],
        out_specs=pl.BlockSpec((tm, d), row),
        out_shape=jax.ShapeDtypeStruct((t, d), F32),
        compiler_params=_params("parallel"),
        name="merge",
    )(a, hf, hb, z, c, dd, z, z, z, z, x, mod, wb, wo)


def _ffn_up_kernel(xp_ref, x_ref, xn_ref, g_ref, mod_ref, wg_ref, wv_ref, dwg_ref, dwv_ref, o_ref,
                   h_ref, ug_ref, uv_ref, *, tm, n_ctx, n_split):
    i = pl.program_id(0)

    @pl.when(pl.program_id(1) == 0)
    def _():
        def norm_mod(x, first_row):
            y = x * lax.rsqrt(jnp.mean(x * x, axis=-1, keepdims=True) + EPS) * g_ref[...]
            rows = first_row + lax.broadcasted_iota(jnp.int32, (x.shape[0], 1), 0)
            is_ctx = rows < n_ctx
            shift = jnp.where(is_ctx, mod_ref[0, 1, 3:4, :], mod_ref[0, 0, 3:4, :])
            scale = jnp.where(is_ctx, mod_ref[0, 1, 4:5, :], mod_ref[0, 0, 4:5, :])
            return y * (1.0 + scale) + shift

        prev_ok = (i > 0).astype(F32)
        next_ok = (i < pl.num_programs(0) - 1).astype(F32)
        h_ref[0:HALO, :] = (norm_mod(xp_ref[...], i * tm - HALO) * prev_ok).astype(BF16)
        h_ref[HALO:HALO + tm, :] = norm_mod(x_ref[...], i * tm).astype(BF16)
        h_ref[HALO + tm:2 * HALO + tm, :] = (norm_mod(xn_ref[...], (i + 1) * tm) * next_ok).astype(BF16)

    rows = i * tm + lax.broadcasted_iota(jnp.int32, (tm, 1), 0)
    keep_prev = (rows != n_ctx).astype(F32)
    keep_next = (rows != n_ctx - 1).astype(F32)

    def conv(u_ref, w_ref, dw_ref, r0, nr):
        u_ref[0:nr + 2 * HALO, :] = _dot(h_ref[r0:r0 + nr + 2 * HALO, :], w_ref[...])
        prev = u_ref[HALO - 1:HALO - 1 + nr, :] * keep_prev[r0:r0 + nr]
        nxt = u_ref[HALO + 1:HALO + 1 + nr, :] * keep_next[r0:r0 + nr]
        return dw_ref[0:1, :] * prev + dw_ref[1:2, :] * u_ref[HALO:HALO + nr, :] + dw_ref[2:3, :] * nxt

    nr = tm // n_split
    for rb in range(n_split):
        ug = conv(ug_ref.at[rb], wg_ref, dwg_ref, rb * nr, nr)
        uv = conv(uv_ref.at[rb], wv_ref, dwv_ref, rb * nr, nr)
        o_ref[rb * nr:(rb + 1) * nr, :] = (ug * _sigmoid(ug) * uv).astype(o_ref.dtype)


def _ffn_up(x, g, mod, layer, w_up, ffn_dw, n_ctx):
    t, d = x.shape
    dff = w_up.shape[2] // 2
    tm = _pick(t, (768, 256))
    tn = _pick(dff, (512, 256))
    nj = dff // tn
    hpt = tm // HALO
    nh = t // HALO
    n_split = 1
    return pl.pallas_call(
        functools.partial(_ffn_up_kernel, tm=tm, n_ctx=n_ctx, n_split=n_split),
        grid=(t // tm, nj),
        in_specs=[pl.BlockSpec((HALO, d), lambda i, j: (jnp.maximum(i * hpt - 1, 0), 0)),
                  pl.BlockSpec((tm, d), lambda i, j: (i, 0)),
                  pl.BlockSpec((HALO, d), lambda i, j: (jnp.minimum((i + 1) * hpt, nh - 1), 0)),
                  pl.BlockSpec((1, d), lambda i, j: (0, 0)),
                  pl.BlockSpec((1, 2, 8, d), lambda i, j: (layer, 0, 0, 0)),
                  pl.BlockSpec((None, d, tn), lambda i, j: (layer, 0, j)),
                  pl.BlockSpec((None, d, tn), lambda i, j: (layer, 0, j + nj)),
                  pl.BlockSpec((3, tn), lambda i, j: (0, j)),
                  pl.BlockSpec((3, tn), lambda i, j: (0, j + nj))],
        out_specs=pl.BlockSpec((tm, tn), lambda i, j: (i, j)),
        out_shape=jax.ShapeDtypeStruct((t, dff), BF16),
        scratch_shapes=[pltpu.VMEM((tm + 2 * HALO, d), BF16),
                        pltpu.VMEM((n_split, tm // n_split + 2 * HALO, tn), F32),
                        pltpu.VMEM((n_split, tm // n_split + 2 * HALO, tn), F32)],
        compiler_params=_params("parallel", "arbitrary"),
        name="ffn_up",
    )(x, x, x, g.reshape(1, d), mod, w_up, w_up, ffn_dw, ffn_dw)


def _ffn_down_kernel(a_ref, w_ref, x_ref, mod_ref, o_ref, *, tm, n_ctx):
    rows = pl.program_id(0) * tm + lax.broadcasted_iota(jnp.int32, (tm, 1), 0)
    gate = jnp.where(rows < n_ctx, mod_ref[0, 1, 5:6, :], mod_ref[0, 0, 5:6, :])
    o_ref[...] = x_ref[...] + gate * _dot(a_ref[...], w_ref[...])


def _ffn_down(act, wd, x, mod, layer, n_ctx):
    t, d = x.shape
    dff = wd.shape[1]
    tm = _pick(t, (768, 256))
    tn = 512
    return pl.pallas_call(
        functools.partial(_ffn_down_kernel, tm=tm, n_ctx=n_ctx),
        grid=(t // tm, d // tn),
        in_specs=[pl.BlockSpec((tm, dff), lambda i, j: (i, 0)),
                  pl.BlockSpec((None, dff, tn), lambda i, j: (layer, 0, j)),
                  pl.BlockSpec((tm, tn), lambda i, j: (i, j)),
                  pl.BlockSpec((1, 2, 8, tn), lambda i, j: (layer, 0, 0, j))],
        out_specs=pl.BlockSpec((tm, tn), lambda i, j: (i, j)),
        out_shape=jax.ShapeDtypeStruct((t, d), F32),
        compiler_params=_params("parallel", "arbitrary"),
        name="ffn_down",
    )(act, wd, x, mod)


def _final_kernel(x_ref, g_ref, o_ref):
    x = x_ref[...]
    o_ref[...] = x * lax.rsqrt(jnp.mean(x * x, axis=-1, keepdims=True) + EPS) * g_ref[...]


def _final_norm(x, g, n_ctx):
    t, d = x.shape
    tm = ROW_TILE
    skip = n_ctx // tm
    return pl.pallas_call(
        _final_kernel,
        grid=((t - n_ctx) // tm,),
        in_specs=[pl.BlockSpec((tm, d), lambda i: (i + skip, 0)),
                  pl.BlockSpec((1, d), lambda i: (0, 0))],
        out_specs=pl.BlockSpec((tm, d), lambda i: (i, 0)),
        out_shape=jax.ShapeDtypeStruct((t - n_ctx, d), F32),
        compiler_params=_params("parallel"),
        name="final_norm",
    )(x, g.reshape(1, d))


def _block_diag(w):
    nb, bi, bj = w.shape
    eye = jnp.eye(nb, dtype=w.dtype)
    return jnp.einsum('nij,nm->nimj', w, eye).reshape(nb * bi, nb * bj)


def kernel(x, c, ctx, c_ctx, w_ada, b_ada, g_mix, w_in, na_rpb, rg_conv, w_rg, b_rg, rg_lambda, conf_dw, conf_ln_g,
           conf_ln_b, q_norm_g, k_norm_g, w_branch, w_out, g_ffn, w_up, ffn_dw, w_down, g_final):
    bsz, seq, d = x.shape
    n_ctx = ctx.shape[1]
    assert bsz == 1 and d == D_MODEL and n_ctx == ROW_TILE and seq % ROW_TILE == 0
    assert seq // GRID_W >= NA_KEY_ROWS

    cc = jnp.concatenate([c, c_ctx[None, :], jnp.zeros((6, d), F32)], axis=0)
    mod = _ada(cc, w_ada, b_ada)[:, :2, :].reshape(DEPTH, 2, N_MOD, d)
    mod = jnp.pad(mod, ((0, 0), (0, 0), (0, 8 - N_MOD), (0, 0)))

    xs = jnp.concatenate([ctx[0], x[0]], axis=0)
    cos, sin = _rope_tables(seq, n_ctx)

    w_in_b = jnp.concatenate([w_in[:, :, N_REST:], w_in[:, :, :N_REST]], axis=2).astype(BF16)
    w_branch_b, w_out_b, w_up_b, w_down_b = (w.astype(BF16) for w in (w_branch, w_out, w_up, w_down))

    for l in range(DEPTH):
        z = _proj(xs, g_mix[l], mod, l, 0, w_in_b, n_ctx, (1792, 896, 256))

        wbd = jnp.stack([jnp.concatenate([_block_diag(w_rg[l, dd, 0]), _block_diag(w_rg[l, dd, 1])], axis=1)
                         for dd in range(2)]).astype(BF16)
        brg = b_rg[l].reshape(2, 1, 2 * BRANCH_W)
        hf, hb = _rglru(z, rg_conv[l], wbd, brg, rg_lambda[l].reshape(2, 1, BRANCH_W))
        c_br = _conformer(z, conf_dw[l], conf_ln_g[l], conf_ln_b[l])
        a_br = _natten(z, _na_bias_table(na_rpb[l]))
        qn, kt, va = _qkprep(z, cos, sin, q_norm_g[l], k_norm_g[l])
        d_br = _gqa(qn, kt, va, n_ctx)
        xs = _merge(a_br, hf, hb, z, c_br, d_br, xs, mod, l, w_branch_b, w_out_b, n_ctx)

        act = _ffn_up(xs, g_ffn[l], mod, l, w_up_b, ffn_dw[l], n_ctx)
        xs = _ffn_down(act, w_down_b, xs, mod, l, n_ctx)

    return _final_norm(xs, g_final, n_ctx)[None]
```

```python
import functools

import numpy as np
import jax
import jax.numpy as jnp
from jax import lax
from jax.experimental import pallas as pl
from jax.experimental.pallas import tpu as pltpu

F32 = jnp.float32
BF16 = jnp.bfloat16

D_MODEL = 2048
DEPTH = 2
GRID_W = 64
HEAD_DIM = 64
N_BRANCH = 4
BRANCH_W = D_MODEL // N_BRANCH
WIN_ROWS = 8
WIN_COLS = 16
RG_C = 8.0
RG_CONV = 4
CONF_WIDTH = 31
GQA_KV_HEADS = 2
ROPE_THETA = 10000.0
D_FF = 5632
EPS = 1e-6
N_MOD = 6
N_GATE = N_BRANCH * D_MODEL
N_REST = 3 * BRANCH_W + 2 * BRANCH_W + 2 * BRANCH_W + BRANCH_W + 2 * GQA_KV_HEADS * HEAD_DIM
N_IN = N_GATE + N_REST

ROW_TILE = 256
HALO = 16
NA_ROWS = ROW_TILE // GRID_W
NA_KEY_ROWS = 3 * NA_ROWS
NEG = -1e30
VMEM_LIMIT = 56 * 1024 * 1024

COL_QA, COL_KA, COL_VA, COL_XR, COL_GR, COL_CV, COL_CG, COL_QD = (N_GATE // BRANCH_W + n for n in range(8))
COL_KD = (N_GATE + 8 * BRANCH_W) // 128
COL_VD = COL_KD + 1


def _params(*sem):
    return pltpu.CompilerParams(dimension_semantics=sem, vmem_limit_bytes=VMEM_LIMIT)


def _pick(total, candidates):
    for cand in candidates:
        if total % cand == 0:
            return cand
    raise ValueError(f"no tile for {total} in {candidates}")


def _dot(a, b):
    return jnp.dot(a, b, preferred_element_type=F32)


def _dot_nt(a, b):
    return lax.dot_general(a, b, (((1,), (1,)), ((), ())), preferred_element_type=F32)


def _sigmoid(x):
    return jax.nn.sigmoid(x)


def _ada_kernel(cc_ref, w_ref, b_ref, o_ref):
    a = cc_ref[...]
    a = a * _sigmoid(a)
    o_ref[0] = _dot(a.astype(BF16), w_ref[0].astype(BF16)) + b_ref[0]


def _ada(cc, w_ada, b_ada):
    depth, d, n = w_ada.shape
    tn = 1024
    return pl.pallas_call(
        _ada_kernel,
        grid=(depth, n // tn),
        in_specs=[pl.BlockSpec((8, d), lambda l, j: (0, 0)),
                  pl.BlockSpec((1, d, tn), lambda l, j: (l, 0, j)),
                  pl.BlockSpec((1, 1, tn), lambda l, j: (l, 0, j))],
        out_specs=pl.BlockSpec((1, 8, tn), lambda l, j: (l, 0, j)),
        out_shape=jax.ShapeDtypeStruct((depth, 8, n), F32),
        compiler_params=_params("parallel", "parallel"),
        name="ada",
    )(cc, w_ada, b_ada.reshape(depth, 1, n))


NORM_ROWS = 64


def _norm_modulate(x_ref, h_ref, h_row0, g_ref, mod_ref, shift_row, first_row, n_ctx, keep=None):
    nrows = x_ref.shape[0]
    chunk = min(NORM_ROWS, nrows)
    gain = [g_ref[...] * (1.0 + mod_ref[0, grp, shift_row + 1:shift_row + 2, :]) for grp in range(2)]
    shift = [mod_ref[0, grp, shift_row:shift_row + 1, :] for grp in range(2)]
    if keep is not None:
        gain = [v * keep for v in gain]
        shift = [v * keep for v in shift]

    def body(r, carry):
        r0 = pl.multiple_of(r * chunk, chunk)
        x = x_ref[pl.ds(r0, chunk), :]
        y = x * lax.rsqrt(jnp.mean(x * x, axis=-1, keepdims=True) + EPS)
        is_ctx = first_row + r0 + lax.broadcasted_iota(jnp.int32, (chunk, 1), 0) < n_ctx
        h = y * jnp.where(is_ctx, gain[1], gain[0]) + jnp.where(is_ctx, shift[1], shift[0])
        h_ref[pl.ds(h_row0 + r0, chunk), :] = h.astype(BF16)
        return carry

    lax.fori_loop(0, nrows // chunk, body, 0)


def _proj_kernel(x_ref, g_ref, mod_ref, w_ref, o_ref, h_ref, *, tm, n_ctx, shift_row):
    @pl.when(pl.program_id(1) == 0)
    def _():
        _norm_modulate(x_ref, h_ref, 0, g_ref, mod_ref, shift_row, pl.program_id(0) * tm, n_ctx)

    o_ref[...] = _dot(h_ref[...], w_ref[...]).astype(o_ref.dtype)


def _proj(x, g, mod, layer, shift_row, w, n_ctx, tn_candidates):
    t, d = x.shape
    n = w.shape[2]
    tm = _pick(t, (768, 256))
    tn = _pick(n, tn_candidates)
    return pl.pallas_call(
        functools.partial(_proj_kernel, tm=tm, n_ctx=n_ctx, shift_row=shift_row),
        grid=(t // tm, n // tn),
        in_specs=[pl.BlockSpec((tm, d), lambda i, j: (i, 0)),
                  pl.BlockSpec((1, d), lambda i, j: (0, 0)),
                  pl.BlockSpec((1, 2, 8, d), lambda i, j: (layer, 0, 0, 0)),
                  pl.BlockSpec((None, d, tn), lambda i, j: (layer, 0, j))],
        out_specs=pl.BlockSpec((tm, tn), lambda i, j: (i, j)),
        out_shape=jax.ShapeDtypeStruct((t, n), BF16),
        scratch_shapes=[pltpu.VMEM((tm, d), BF16)],
        compiler_params=_params("parallel", "arbitrary"),
        name="proj",
    )(x, g.reshape(1, d), mod, w)


def _softplus(x):
    return jnp.maximum(x, 0.0) + jnp.log(1.0 + jnp.exp(-jnp.abs(x)))


N_SEG = 8


def _scan_rows(a_ref, b_ref, h_ref, carry_ref, reverse):
    nblk, _, width = a_ref.shape
    seg_len = h_ref.shape[0] // N_SEG
    pitch = _seg_pitch(seg_len)
    for c in range(nblk):
        cols = slice(c * width, (c + 1) * width)
        h = jnp.zeros((N_SEG, width), F32)
        p = jnp.ones((N_SEG, width), F32)
        for j in (reversed(range(seg_len)) if reverse else range(seg_len)):
            rows = pl.ds(j, N_SEG, stride=pitch)
            aj = a_ref[c, rows, :]
            h = aj * h + b_ref[c, rows, :]
            p = aj * p
            b_ref[c, rows, :] = h
            a_ref[c, rows, :] = p
        state = carry_ref[0:1, cols]
        for k in (reversed(range(N_SEG)) if reverse else range(N_SEG)):
            h_ref[k * seg_len:(k + 1) * seg_len, cols] = (
                b_ref[c, k * pitch:k * pitch + seg_len, :] + a_ref[c, k * pitch:k * pitch + seg_len, :] * state)
            state = h[k:k + 1, :] + p[k:k + 1, :] * state
        carry_ref[:, cols] = jnp.broadcast_to(state, (carry_ref.shape[0], width))


def _seg_pitch(seg_len):
    return seg_len + 8 if (seg_len // 8) % 2 == 0 else seg_len


def _rg_kernel(xfp, xfm, xfn, xbp, xbm, xbn, cw_ref, wbd_ref, brg_ref, lam_ref, hf_ref, hb_ref,
               ext, a_ref, b_ref, carry_f, carry_b, *, nchunks):
    s = pl.program_id(0)
    tm = xfm.shape[0]
    bw = xfm.shape[1]

    @pl.when(s == 0)
    def _():
        carry_f[...] = jnp.zeros_like(carry_f)
        carry_b[...] = jnp.zeros_like(carry_b)

    cb = jnp.where(s == 0, 0, nchunks - s)

    def gates(d, xp, xm, xn, cidx):
        prev_ok = (cidx >= 2).astype(F32)
        next_ok = jnp.logical_and(cidx >= 1, cidx <= nchunks - 2).astype(F32)
        ext[0:HALO, :] = xp[...].astype(F32) * prev_ok
        ext[HALO:HALO + tm, :] = xm[...].astype(F32)
        ext[HALO + tm:2 * HALO + tm, :] = xn[...].astype(F32) * next_ok
        left = RG_CONV // 2
        xl = cw_ref[0:1, :] * ext[HALO - left:HALO - left + tm, :]
        for k in range(1, RG_CONV):
            xl = xl + cw_ref[k:k + 1, :] * ext[HALO - left + k:HALO - left + k + tm, :]
        g = _sigmoid(_dot(xl.astype(BF16), wbd_ref[d]) + brg_ref[d])
        log_a = (-RG_C * g[:, :bw]) * _softplus(-lam_ref[d])
        a = jnp.exp(log_a)
        b = jnp.sqrt(-jnp.tanh(log_a) * (a * a + 1.0)) * g[:, bw:] * xl
        seg_len = tm // N_SEG
        pitch = _seg_pitch(seg_len)
        for c in range(bw // 128):
            for k in range(N_SEG):
                a_ref[d, c, k * pitch:k * pitch + seg_len, :] = a[k * seg_len:(k + 1) * seg_len, 128 * c:128 * (c + 1)]
                b_ref[d, c, k * pitch:k * pitch + seg_len, :] = b[k * seg_len:(k + 1) * seg_len, 128 * c:128 * (c + 1)]

    gates(0, xfp, xfm, xfn, s)
    gates(1, xbp, xbm, xbn, cb)
    _scan_rows(a_ref.at[0], b_ref.at[0], hf_ref, carry_f, reverse=False)
    _scan_rows(a_ref.at[1], b_ref.at[1], hb_ref, carry_b, reverse=True)


def _rglru(z, rg_conv, wbd, brg, lam):
    t = z.shape[0]
    tm = ROW_TILE
    nchunks = t // tm
    hpt = tm // HALO
    nh = t // HALO

    def bwd(s):
        return jnp.where(s == 0, 0, nchunks - s)

    def prev_map(cidx):
        return lambda s: (jnp.maximum(cidx(s) * hpt - 1, 0), COL_XR)

    def next_map(cidx):
        return lambda s: (jnp.minimum((cidx(s) + 1) * hpt, nh - 1), COL_XR)

    fwd = lambda s: s
    in_specs = [
        pl.BlockSpec((HALO, BRANCH_W), prev_map(fwd)),
        pl.BlockSpec((tm, BRANCH_W), lambda s: (s, COL_XR)),
        pl.BlockSpec((HALO, BRANCH_W), next_map(fwd)),
        pl.BlockSpec((HALO, BRANCH_W), prev_map(bwd)),
        pl.BlockSpec((tm, BRANCH_W), lambda s: (bwd(s), COL_XR)),
        pl.BlockSpec((HALO, BRANCH_W), next_map(bwd)),
        pl.BlockSpec((RG_CONV, BRANCH_W), lambda s: (0, 0)),
        pl.BlockSpec((2, BRANCH_W, 2 * BRANCH_W), lambda s: (0, 0, 0)),
        pl.BlockSpec((2, 1, 2 * BRANCH_W), lambda s: (0, 0, 0)),
        pl.BlockSpec((2, 1, BRANCH_W), lambda s: (0, 0, 0)),
    ]
    return pl.pallas_call(
        functools.partial(_rg_kernel, nchunks=nchunks),
        grid=(nchunks,),
        in_specs=in_specs,
        out_specs=[pl.BlockSpec((tm, BRANCH_W), lambda s: (s, 0)),
                   pl.BlockSpec((tm, BRANCH_W), lambda s: (bwd(s), 0))],
        out_shape=[jax.ShapeDtypeStruct((t, BRANCH_W), F32)] * 2,
        scratch_shapes=[pltpu.VMEM((tm + 2 * HALO, BRANCH_W), F32),
                        pltpu.VMEM((2, BRANCH_W // 128, N_SEG * _seg_pitch(tm // N_SEG), 128), F32),
                        pltpu.VMEM((2, BRANCH_W // 128, N_SEG * _seg_pitch(tm // N_SEG), 128), F32),
                        pltpu.VMEM((8, BRANCH_W), F32),
                        pltpu.VMEM((8, BRANCH_W), F32)],
        compiler_params=_params("arbitrary"),
        name="rglru",
    )(z, z, z, z, z, z, rg_conv, wbd, brg, lam)


def _conf_kernel(vp, vm, vn, gp, gm, gn, dw_ref, lng_ref, lnb_ref, o_ref, ext, shifted, *, nchunks):
    i = pl.program_id(0)
    tm = vm.shape[0]
    prev_ok = (i >= 2).astype(F32)
    next_ok = jnp.logical_and(i >= 1, i <= nchunks - 2).astype(F32)

    def glu(v, g):
        return v[...].astype(F32) * _sigmoid(g[...].astype(F32))

    ext[0:HALO, :] = glu(vp, gp) * prev_ok
    ext[HALO:HALO + tm, :] = glu(vm, gm)
    ext[HALO + tm:2 * HALO + tm, :] = glu(vn, gn) * next_ok
    base = HALO - CONF_WIDTH // 2
    span = tm + 2 * HALO - 8
    for r in range(1, 8):
        shifted[r - 1] = ext[r:r + span, :]
    acc = None
    for k in range(CONF_WIDTH):
        q, r = divmod(base + k, 8)
        rows = ext[8 * q:8 * q + tm, :] if r == 0 else shifted[r - 1, 8 * q:8 * q + tm, :]
        term = dw_ref[k:k + 1, :] * rows
        acc = term if acc is None else acc + term
    mu = jnp.mean(acc, axis=-1, keepdims=True)
    xc = acc - mu
    y = xc * lax.rsqrt(jnp.mean(xc * xc, axis=-1, keepdims=True) + EPS) * lng_ref[...] + lnb_ref[...]
    o_ref[...] = (y * _sigmoid(y)).astype(o_ref.dtype)


def _conformer(z, conf_dw, ln_g, ln_b):
    t = z.shape[0]
    tm = ROW_TILE
    nchunks = t // tm
    hpt = tm // HALO
    nh = t // HALO

    def specs(col):
        return [pl.BlockSpec((HALO, BRANCH_W), lambda i: (jnp.maximum(i * hpt - 1, 0), col)),
                pl.BlockSpec((tm, BRANCH_W), lambda i: (i, col)),
                pl.BlockSpec((HALO, BRANCH_W), lambda i: (jnp.minimum((i + 1) * hpt, nh - 1), col))]

    return pl.pallas_call(
        functools.partial(_conf_kernel, nchunks=nchunks),
        grid=(nchunks,),
        in_specs=specs(COL_CV) + specs(COL_CG) + [
            pl.BlockSpec((CONF_WIDTH, BRANCH_W), lambda i: (0, 0)),
            pl.BlockSpec((1, BRANCH_W), lambda i: (0, 0)),
            pl.BlockSpec((1, BRANCH_W), lambda i: (0, 0))],
        out_specs=pl.BlockSpec((tm, BRANCH_W), lambda i: (i, 0)),
        out_shape=jax.ShapeDtypeStruct((t, BRANCH_W), BF16),
        scratch_shapes=[pltpu.VMEM((tm + 2 * HALO, BRANCH_W), F32),
                        pltpu.VMEM((7, tm + 2 * HALO - 8, BRANCH_W), F32)],
        compiler_params=_params("parallel"),
        name="conformer",
    )(z, z, z, z, z, z, conf_dw, ln_g.reshape(1, -1), ln_b.reshape(1, -1))


def _na_bias_table(rpb):
    i = np.arange(NA_ROWS)[:, None]
    j = np.arange(NA_KEY_ROWS)[None, :]
    variants = [(np.zeros_like(i), i), (i, NA_ROWS + i), (np.full_like(i, NA_ROWS), 2 * NA_ROWS + i)]
    c = np.arange(GRID_W)[:, None]
    kc = np.arange(GRID_W)[None, :]
    col_start = np.clip(c - WIN_COLS // 2, 0, GRID_W - WIN_COLS)
    col_ok = (kc >= col_start) & (kc < col_start + WIN_COLS)
    nh = rpb.shape[0]
    padded = jnp.pad(rpb.astype(F32), ((0, 0), (0, 0), (GRID_W, GRID_W)))
    off = GRID_W + WIN_COLS - 1
    toeplitz = jnp.stack([padded[:, :, off - cc:off - cc + GRID_W] for cc in range(GRID_W)], axis=2)
    toeplitz = jnp.where(col_ok[None, None], toeplitz, NEG)
    masked = jnp.full((nh, GRID_W, GRID_W), NEG, F32)
    tables = [jnp.full((nh, ROW_TILE, NA_KEY_ROWS * GRID_W), NEG, F32)]
    for start, qrow in variants:
        row_ok = (j >= start) & (j < start + WIN_ROWS)
        dr = j - qrow + WIN_ROWS - 1
        rows = [jnp.concatenate([toeplitz[:, dr[qi, kj]] if row_ok[qi, kj] else masked for kj in range(NA_KEY_ROWS)],
                                axis=-1) for qi in range(NA_ROWS)]
        tables.append(jnp.stack(rows, axis=1).reshape(nh, ROW_TILE, NA_KEY_ROWS * GRID_W))
    return jnp.stack(tables)


def _na_kernel(q_ref, k0, k1, k2, kc, v0, v1, v2, vc, bias_ref, o_ref):
    tm = q_ref.shape[0]
    lane = lax.broadcasted_iota(jnp.int32, (1, 128), 1)
    scale = HEAD_DIM ** -0.5
    qmask = [jnp.where(lane < HEAD_DIM, scale, 0.0).astype(BF16), jnp.where(lane >= HEAD_DIM, scale, 0.0).astype(BF16)]
    first_half = lax.broadcasted_iota(jnp.int32, (tm, 128), 1) < HEAD_DIM
    one = jnp.ones((tm, 128), BF16)

    def scores(h):
        cols = slice(128 * (h // 2), 128 * (h // 2 + 1))
        qm = q_ref[:, cols] * qmask[h % 2]
        s = [_dot_nt(qm, r[:, cols]) for r in (k0, k1, k2, kc)]
        return [s[jb] + bias_ref[0, h, :, tm * jb:tm * (jb + 1)] for jb in range(3)] + [s[3]]

    n_heads = BRANCH_W // HEAD_DIM
    s_next = scores(0)
    pair = []
    for h in range(n_heads):
        s = s_next
        if h + 1 < n_heads:
            s_next = scores(h + 1)
        cols = slice(128 * (h // 2), 128 * (h // 2 + 1))
        m = jnp.max(jnp.maximum(jnp.maximum(s[0], s[1]), jnp.maximum(s[2], s[3])), axis=-1, keepdims=True)
        o = None
        for sj, r in zip(s, (v0, v1, v2, vc)):
            va = jnp.where(first_half, r[:, cols], one) if h % 2 == 0 else jnp.where(first_half, one, r[:, cols])
            term = _dot(jnp.exp(sj - m).astype(BF16), va)
            o = term if o is None else o + term
        pair.append(o)
        if h % 2 == 1:
            num = jnp.where(first_half, pair[0], pair[1])
            den = jnp.where(first_half, pltpu.roll(pair[0], HEAD_DIM, 1), pltpu.roll(pair[1], HEAD_DIM, 1))
            o_ref[:, cols] = (num * (1.0 / den)).astype(o_ref.dtype)
            pair = []


def _natten(z, bias):
    t = z.shape[0]
    tm = ROW_TILE
    nt = t // tm
    assert nt >= 4

    def kbase(i):
        return jnp.clip(i - 1, 1, nt - 3)

    def variant(i):
        return jnp.where(i == 0, 0, jnp.where(i == 1, 1, jnp.where(i == nt - 1, 3, 2)))

    def kv_specs(col):
        return [pl.BlockSpec((tm, BRANCH_W), lambda i: (kbase(i), col)),
                pl.BlockSpec((tm, BRANCH_W), lambda i: (kbase(i) + 1, col)),
                pl.BlockSpec((tm, BRANCH_W), lambda i: (kbase(i) + 2, col)),
                pl.BlockSpec((tm, BRANCH_W), lambda i: (0, col))]

    nh = bias.shape[1]
    return pl.pallas_call(
        _na_kernel,
        grid=(nt,),
        in_specs=[pl.BlockSpec((tm, BRANCH_W), lambda i: (i, COL_QA))] + kv_specs(COL_KA) + kv_specs(COL_VA) + [
            pl.BlockSpec((1, nh, tm, 3 * tm), lambda i: (variant(i), 0, 0, 0))],
        out_specs=pl.BlockSpec((tm, BRANCH_W), lambda i: (i, 0)),
        out_shape=jax.ShapeDtypeStruct((t, BRANCH_W), BF16),
        compiler_params=_params("parallel"),
        name="natten",
    )(z, z, z, z, z, z, z, z, z, bias)


def _rope_tables(seq, n_ctx):
    half = HEAD_DIM // 2
    n_f = half // 2
    inv_freq = ROPE_THETA ** (-jnp.arange(n_f, dtype=F32) / n_f)
    pos = jnp.arange(seq)
    ang_r = (pos // GRID_W).astype(F32)[:, None] * inv_freq[None, :]
    ang_c = (pos % GRID_W).astype(F32)[:, None] * inv_freq[None, :]
    cos = jnp.concatenate([jnp.cos(ang_r), jnp.cos(ang_r), jnp.cos(ang_c), jnp.cos(ang_c)], axis=-1)
    sin = jnp.concatenate([-jnp.sin(ang_r), jnp.sin(ang_r), -jnp.sin(ang_c), jnp.sin(ang_c)], axis=-1)
    cos = jnp.concatenate([jnp.ones((n_ctx, HEAD_DIM), F32), cos], axis=0)
    sin = jnp.concatenate([jnp.zeros((n_ctx, HEAD_DIM), F32), sin], axis=0)
    return jnp.tile(cos, (1, 2)), jnp.tile(sin, (1, 2))


def _qkprep_kernel(q_ref, k_ref, v_ref, cos_ref, sin_ref, qg_ref, kg_ref, aq_ref, ak_ref, qo_ref, ko_ref, vo_ref):
    n_f = HEAD_DIM // 4

    def norm_rope(x, g, avg, cos, sin):
        n = x.shape[1]
        x2 = x * x
        hi = x2.astype(BF16)
        lo = (x2 - hi.astype(F32)).astype(BF16)
        ms = _dot(hi, avg) + _dot(lo, avg)
        y = x * lax.rsqrt(ms + EPS) * g
        lane = lax.broadcasted_iota(jnp.int32, x.shape, 1) % (2 * n_f)
        partner = jnp.where(lane < n_f, pltpu.roll(y, n - n_f, 1), pltpu.roll(y, n_f, 1))
        return y * cos + partner * sin

    cos = cos_ref[...]
    sin = sin_ref[...]
    reps = q_ref.shape[1] // 128
    q = norm_rope(q_ref[...].astype(F32), qg_ref[...], aq_ref[...],
                  jnp.concatenate([cos] * reps, axis=1), jnp.concatenate([sin] * reps, axis=1))
    qo_ref[...] = (q * HEAD_DIM ** -0.5).astype(BF16)

    first = lax.broadcasted_iota(jnp.int32, k_ref.shape, 1) < HEAD_DIM
    k = norm_rope(k_ref[...].astype(F32), kg_ref[...], ak_ref[...], cos, sin)
    k_sw = pltpu.roll(k, HEAD_DIM, 1)
    kdup = jnp.concatenate([jnp.where(first, k, k_sw), jnp.where(first, k_sw, k)], axis=1)
    ko_ref[...] = kdup.T.astype(BF16)
    v = v_ref[...].astype(F32)
    v_sw = pltpu.roll(v, HEAD_DIM, 1)
    vo_ref[...] = jnp.concatenate([jnp.where(first, v, 1.0), jnp.where(first, 1.0, v_sw),
                                   jnp.where(first, v_sw, 1.0), jnp.where(first, 1.0, v)], axis=1).astype(BF16)


def _head_avg(n):
    idx = np.arange(n) // HEAD_DIM
    return jnp.asarray((idx[:, None] == idx[None, :]).astype(np.float32) / HEAD_DIM, BF16)


def _qkprep(z, cos, sin, q_norm_g, k_norm_g):
    t = z.shape[0]
    tm = ROW_TILE
    kvw = GQA_KV_HEADS * HEAD_DIM
    qg = jnp.tile(q_norm_g.astype(F32), BRANCH_W // HEAD_DIM).reshape(1, BRANCH_W)
    kg = jnp.tile(k_norm_g.astype(F32), GQA_KV_HEADS).reshape(1, kvw)
    const = lambda i: (0, 0)
    return pl.pallas_call(
        _qkprep_kernel,
        grid=(t // tm,),
        in_specs=[pl.BlockSpec((tm, BRANCH_W), lambda i: (i, COL_QD)),
                  pl.BlockSpec((tm, kvw), lambda i: (i, COL_KD)),
                  pl.BlockSpec((tm, kvw), lambda i: (i, COL_VD)),
                  pl.BlockSpec((tm, kvw), lambda i: (i, 0)),
                  pl.BlockSpec((tm, kvw), lambda i: (i, 0)),
                  pl.BlockSpec((1, BRANCH_W), const),
                  pl.BlockSpec((1, kvw), const),
                  pl.BlockSpec((BRANCH_W, BRANCH_W), const),
                  pl.BlockSpec((kvw, kvw), const)],
        out_specs=[pl.BlockSpec((tm, BRANCH_W), lambda i: (i, 0)),
                   pl.BlockSpec((2 * kvw, tm), lambda i: (0, i)),
                   pl.BlockSpec((tm, 4 * kvw), lambda i: (i, 0))],
        out_shape=[jax.ShapeDtypeStruct((t, BRANCH_W), BF16),
                   jax.ShapeDtypeStruct((2 * kvw, t), BF16),
                   jax.ShapeDtypeStruct((t, 4 * kvw), BF16)],
        compiler_params=_params("parallel"),
        name="qkprep",
    )(z, z, z, cos, sin, qg, kg, _head_avg(BRANCH_W), _head_avg(kvw))


def _gqa_kernel(q_ref, kt_ref, va_ref, o_ref, qs_ref, s_ref, m_ref, acc_ref, *, n_ctx, n_chunks, tk):
    i = pl.program_id(0)
    tq = q_ref.shape[0]
    lane = lax.broadcasted_iota(jnp.int32, (1, 128), 1)
    hmask = [(lane < HEAD_DIM).astype(BF16), (lane >= HEAD_DIM).astype(BF16)]
    first_half = lax.broadcasted_iota(jnp.int32, (tq, 128), 1) < HEAD_DIM
    group = BRANCH_W // HEAD_DIM // GQA_KV_HEADS
    n_trips = jnp.where(i == 0, 0, n_chunks // 2 - 1)
    for g in range(GQA_KV_HEADS):
        for hh in range(group):
            pair = q_ref[:, 128 * (g * group // 2 + hh // 2):128 * (g * group // 2 + hh // 2 + 1)]
            qs_ref[hh] = pair * hmask[hh % 2]

        def scores(off, size, slot):
            for hh in range(group):
                s_ref[slot, hh, :, 0:size] = _dot(qs_ref[hh], kt_ref[128 * g:128 * (g + 1), pl.ds(off, size)])

        def update(off, size, slot, first):
            for hh in range(group):
                s = s_ref[slot, hh, :, 0:size]
                mx = s[:, 0:128]
                for cc in range(1, size // 128):
                    mx = jnp.maximum(mx, s[:, 128 * cc:128 * (cc + 1)])
                rmax = jnp.max(mx, axis=-1, keepdims=True)
                vcol = 256 * g + 128 * (hh % 2)
                va = va_ref[pl.ds(off, size), vcol:vcol + 128]
                if first:
                    m_new = jnp.broadcast_to(rmax, (tq, 128))
                    p = jnp.exp(s - jnp.concatenate([m_new] * (size // 128), axis=1))
                    acc_ref[hh] = _dot(p.astype(BF16), va)
                else:
                    m_old = m_ref[hh]
                    m_new = jnp.maximum(m_old, rmax)
                    p = jnp.exp(s - jnp.concatenate([m_new] * (size // 128), axis=1))
                    acc_ref[hh] = jnp.exp(m_old - m_new) * acc_ref[hh] + _dot(p.astype(BF16), va)
                m_ref[hh] = m_new

        scores(0, n_ctx, 1)
        scores(n_ctx, tk, 0)
        update(0, n_ctx, 1, True)

        def body(c2, carry):
            off = pl.multiple_of(n_ctx + 2 * c2 * tk, 256)
            scores(off + tk, tk, 1)
            update(off, tk, 0, False)
            scores(off + 2 * tk, tk, 0)
            update(off + tk, tk, 1, False)
            return carry

        lax.fori_loop(0, n_trips, body, 0)

        @pl.when(i > 0)
        def _():
            off = n_ctx + (n_chunks - 2) * tk
            scores(off + tk, tk, 1)
            update(off, tk, 0, False)
            update(off + tk, tk, 1, False)

        for pr in range(group // 2):
            o0 = acc_ref[2 * pr]
            o1 = acc_ref[2 * pr + 1]
            num = jnp.where(first_half, o0, o1)
            den = jnp.where(first_half, pltpu.roll(o0, HEAD_DIM, 1), pltpu.roll(o1, HEAD_DIM, 1))
            col = 128 * (g * group // 2 + pr)
            o_ref[:, col:col + 128] = (num * (1.0 / den)).astype(o_ref.dtype)


def _gqa(qn, kt, va, n_ctx):
    t = qn.shape[0]
    tq = ROW_TILE
    assert n_ctx == tq
    tk = _pick(t - n_ctx, (2048, 1024, 512)) // 2
    n_chunks = (t - n_ctx) // tk
    assert tk >= n_ctx
    group = BRANCH_W // HEAD_DIM // GQA_KV_HEADS
    return pl.pallas_call(
        functools.partial(_gqa_kernel, n_ctx=n_ctx, n_chunks=n_chunks, tk=tk),
        grid=(t // tq,),
        in_specs=[pl.BlockSpec((tq, BRANCH_W), lambda i: (i, 0)),
                  pl.BlockSpec(kt.shape, lambda i: (0, 0), pipeline_mode=pl.Buffered(1)),
                  pl.BlockSpec(va.shape, lambda i: (0, 0), pipeline_mode=pl.Buffered(1))],
        out_specs=pl.BlockSpec((tq, BRANCH_W), lambda i: (i, 0)),
        out_shape=jax.ShapeDtypeStruct((t, BRANCH_W), BF16),
        scratch_shapes=[pltpu.VMEM((group, tq, 128), BF16),
                        pltpu.VMEM((2, group, tq, tk), F32),
                        pltpu.VMEM((group, tq, 128), F32),
                        pltpu.VMEM((group, tq, 128), F32)],
        compiler_params=_params("parallel"),
        name="gqa",
    )(qn, kt, va)


def _gelu_tanh(x):
    return 0.5 * x * (1.0 + jnp.tanh(np.sqrt(2.0 / np.pi).astype(np.float32) * (x + 0.044715 * (x * x * x))))


def _merge_kernel(a_ref, hf_ref, hb_ref, gr_ref, c_ref, d_ref, g0, g1, g2, g3, x_ref, mod_ref, wb_ref, wo_ref, o_ref):
    b = ((hf_ref[...] + hb_ref[...]) * _gelu_tanh(gr_ref[...].astype(F32))).astype(BF16)
    branches = (a_ref[...], b, c_ref[...], d_ref[...])
    gates = (g0, g1, g2, g3)
    y = None
    for n in range(N_BRANCH):
        term = _sigmoid(gates[n][...].astype(F32)) * _dot(branches[n], wb_ref[n])
        y = term if y is None else y + term
    out = _dot(y.astype(BF16), wo_ref[...])
    o_ref[...] = x_ref[...] + mod_ref[0, 0, 2:3, :] * out


def _merge(a, hf, hb, z, c, dd, x, mod, layer, wb, wo, n_ctx):
    t, d = x.shape
    tm = ROW_TILE
    nctx_tiles = n_ctx // tm
    row = lambda i: (i, 0)
    single = pl.Buffered(1)
    return pl.pallas_call(
        _merge_kernel,
        grid=(t // tm,),
        in_specs=[pl.BlockSpec((tm, BRANCH_W), row),
                  pl.BlockSpec((tm, BRANCH_W), row),
                  pl.BlockSpec((tm, BRANCH_W), row),
                  pl.BlockSpec((tm, BRANCH_W), lambda i: (i, COL_GR)),
                  pl.BlockSpec((tm, BRANCH_W), row),
                  pl.BlockSpec((tm, BRANCH_W), row)] + [
                      pl.BlockSpec((tm, d), functools.partial(lambda n, i: (i, n), n)) for n in range(N_BRANCH)] + [
                  pl.BlockSpec((tm, d), row),
                  pl.BlockSpec((1, 1, 8, d), lambda i: (layer, (i < nctx_tiles).astype(jnp.int32), 0, 0)),
                  pl.BlockSpec((None, N_BRANCH, BRANCH_W, d), lambda i: (layer, 0, 0, 0), pipeline_mode=single),
                  pl.BlockSpec((None, d, d), lambda i: (layer, 0, 0), pipeline_mode=single)],
        out_specs=pl.BlockSpec((tm, d), row),
        out_shape=jax.ShapeDtypeStruct((t, d), F32),
        compiler_params=_params("parallel"),
        name="merge",
    )(a, hf, hb, z, c, dd, z, z, z, z, x, mod, wb, wo)


def _ffn_up_kernel(xp_ref, x_ref, xn_ref, g_ref, mod_ref, wg_ref, wv_ref, dwg_ref, dwv_ref, o_ref,
                   h_ref, ug_ref, uv_ref, *, tm, n_ctx, n_split):
    i = pl.program_id(0)

    @pl.when(pl.program_id(1) == 0)
    def _():
        prev_ok = (i > 0).astype(F32)
        next_ok = (i < pl.num_programs(0) - 1).astype(F32)
        _norm_modulate(xp_ref, h_ref, 0, g_ref, mod_ref, 3, i * tm - HALO, n_ctx, keep=prev_ok)
        _norm_modulate(x_ref, h_ref, HALO, g_ref, mod_ref, 3, i * tm, n_ctx)
        _norm_modulate(xn_ref, h_ref, HALO + tm, g_ref, mod_ref, 3, (i + 1) * tm, n_ctx, keep=next_ok)

    rows = i * tm + lax.broadcasted_iota(jnp.int32, (tm, 1), 0)
    keep_prev = (rows != n_ctx).astype(F32)
    keep_next = (rows != n_ctx - 1).astype(F32)

    def conv(u_ref, w_ref, dw_ref, cols):
        u_ref[...] = _dot(h_ref[...], w_ref[:, cols])
        prev = u_ref[HALO - 1:HALO - 1 + tm, :] * keep_prev
        nxt = u_ref[HALO + 1:HALO + 1 + tm, :] * keep_next
        return dw_ref[0:1, cols] * prev + dw_ref[1:2, cols] * u_ref[HALO:HALO + tm, :] + dw_ref[2:3, cols] * nxt

    nc = o_ref.shape[1] // n_split
    for cb in range(n_split):
        cols = slice(cb * nc, (cb + 1) * nc)
        ug = conv(ug_ref.at[cb], wg_ref, dwg_ref, cols)
        uv = conv(uv_ref.at[cb], wv_ref, dwv_ref, cols)
        o_ref[:, cols] = (ug * _sigmoid(ug) * uv).astype(o_ref.dtype)


def _ffn_up(x, g, mod, layer, w_up, ffn_dw, n_ctx):
    t, d = x.shape
    dff = w_up.shape[2] // 2
    tm = _pick(t, (768, 256))
    tn = _pick(dff, (512, 256))
    nj = dff // tn
    hpt = tm // HALO
    nh = t // HALO
    n_split = 1
    return pl.pallas_call(
        functools.partial(_ffn_up_kernel, tm=tm, n_ctx=n_ctx, n_split=n_split),
        grid=(t // tm, nj),
        in_specs=[pl.BlockSpec((HALO, d), lambda i, j: (jnp.maximum(i * hpt - 1, 0), 0)),
                  pl.BlockSpec((tm, d), lambda i, j: (i, 0)),
                  pl.BlockSpec((HALO, d), lambda i, j: (jnp.minimum((i + 1) * hpt, nh - 1), 0)),
                  pl.BlockSpec((1, d), lambda i, j: (0, 0)),
                  pl.BlockSpec((1, 2, 8, d), lambda i, j: (layer, 0, 0, 0)),
                  pl.BlockSpec((None, d, tn), lambda i, j: (layer, 0, j)),
                  pl.BlockSpec((None, d, tn), lambda i, j: (layer, 0, j + nj)),
                  pl.BlockSpec((3, tn), lambda i, j: (0, j)),
                  pl.BlockSpec((3, tn), lambda i, j: (0, j + nj))],
        out_specs=pl.BlockSpec((tm, tn), lambda i, j: (i, j)),
        out_shape=jax.ShapeDtypeStruct((t, dff), BF16),
        scratch_shapes=[pltpu.VMEM((tm + 2 * HALO, d), BF16),
                        pltpu.VMEM((n_split, tm + 2 * HALO, tn // n_split), F32),
                        pltpu.VMEM((n_split, tm + 2 * HALO, tn // n_split), F32)],
        compiler_params=_params("parallel", "arbitrary"),
        name="ffn_up",
    )(x, x, x, g.reshape(1, d), mod, w_up, w_up, ffn_dw, ffn_dw)


def _ffn_down_kernel(a_ref, w_ref, x_ref, mod_ref, o_ref, *, tm, n_ctx):
    rows = pl.program_id(0) * tm + lax.broadcasted_iota(jnp.int32, (tm, 1), 0)
    gate = jnp.where(rows < n_ctx, mod_ref[0, 1, 5:6, :], mod_ref[0, 0, 5:6, :])
    o_ref[...] = x_ref[...] + gate * _dot(a_ref[...], w_ref[...])


def _ffn_down(act, wd, x, mod, layer, n_ctx):
    t, d = x.shape
    dff = wd.shape[1]
    tm = _pick(t, (768, 256))
    tn = 512
    return pl.pallas_call(
        functools.partial(_ffn_down_kernel, tm=tm, n_ctx=n_ctx),
        grid=(t // tm, d // tn),
        in_specs=[pl.BlockSpec((tm, dff), lambda i, j: (i, 0)),
                  pl.BlockSpec((None, dff, tn), lambda i, j: (layer, 0, j)),
                  pl.BlockSpec((tm, tn), lambda i, j: (i, j)),
                  pl.BlockSpec((1, 2, 8, tn), lambda i, j: (layer, 0, 0, j))],
        out_specs=pl.BlockSpec((tm, tn), lambda i, j: (i, j)),
        out_shape=jax.ShapeDtypeStruct((t, d), F32),
        compiler_params=_params("parallel", "arbitrary"),
        name="ffn_down",
    )(act, wd, x, mod)


def _final_kernel(x_ref, g_ref, o_ref):
    x = x_ref[...]
    o_ref[...] = x * lax.rsqrt(jnp.mean(x * x, axis=-1, keepdims=True) + EPS) * g_ref[...]


def _final_norm(x, g, n_ctx):
    t, d = x.shape
    tm = ROW_TILE
    skip = n_ctx // tm
    return pl.pallas_call(
        _final_kernel,
        grid=((t - n_ctx) // tm,),
        in_specs=[pl.BlockSpec((tm, d), lambda i: (i + skip, 0)),
                  pl.BlockSpec((1, d), lambda i: (0, 0))],
        out_specs=pl.BlockSpec((tm, d), lambda i: (i, 0)),
        out_shape=jax.ShapeDtypeStruct((t - n_ctx, d), F32),
        compiler_params=_params("parallel"),
        name="final_norm",
    )(x, g.reshape(1, d))


def _block_diag(w):
    nb, bi, bj = w.shape
    eye = jnp.eye(nb, dtype=w.dtype)
    return jnp.einsum('nij,nm->nimj', w, eye).reshape(nb * bi, nb * bj)


def kernel(x, c, ctx, c_ctx, w_ada, b_ada, g_mix, w_in, na_rpb, rg_conv, w_rg, b_rg, rg_lambda, conf_dw, conf_ln_g,
           conf_ln_b, q_norm_g, k_norm_g, w_branch, w_out, g_ffn, w_up, ffn_dw, w_down, g_final):
    bsz, seq, d = x.shape
    n_ctx = ctx.shape[1]
    assert bsz == 1 and d == D_MODEL and n_ctx == ROW_TILE and seq % ROW_TILE == 0
    assert seq // GRID_W >= NA_KEY_ROWS

    cc = jnp.concatenate([c, c_ctx[None, :], jnp.zeros((6, d), F32)], axis=0)
    mod = _ada(cc, w_ada, b_ada)[:, :2, :].reshape(DEPTH, 2, N_MOD, d)
    mod = jnp.pad(mod, ((0, 0), (0, 0), (0, 8 - N_MOD), (0, 0)))

    xs = jnp.concatenate([ctx[0], x[0]], axis=0)
    cos, sin = _rope_tables(seq, n_ctx)

    w_in_b = jnp.concatenate([w_in[:, :, N_REST:].astype(BF16), w_in[:, :, :N_REST].astype(BF16)], axis=2)
    w_branch_b, w_out_b, w_up_b, w_down_b = (w.astype(BF16) for w in (w_branch, w_out, w_up, w_down))

    for l in range(DEPTH):
        z = _proj(xs, g_mix[l], mod, l, 0, w_in_b, n_ctx, (1792, 896, 256))

        wbd = jnp.stack([jnp.concatenate([_block_diag(w_rg[l, dd, 0]), _block_diag(w_rg[l, dd, 1])], axis=1)
                         for dd in range(2)]).astype(BF16)
        brg = b_rg[l].reshape(2, 1, 2 * BRANCH_W)
        hf, hb = _rglru(z, rg_conv[l], wbd, brg, rg_lambda[l].reshape(2, 1, BRANCH_W))
        c_br = _conformer(z, conf_dw[l], conf_ln_g[l], conf_ln_b[l])
        a_br = _natten(z, _na_bias_table(na_rpb[l]))
        qn, kt, va = _qkprep(z, cos, sin, q_norm_g[l], k_norm_g[l])
        d_br = _gqa(qn, kt, va, n_ctx)
        xs = _merge(a_br, hf, hb, z, c_br, d_br, xs, mod, l, w_branch_b, w_out_b, n_ctx)

        act = _ffn_up(xs, g_ffn[l], mod, l, w_up_b, ffn_dw[l], n_ctx)
        xs = _ffn_down(act, w_down_b, xs, mod, l, n_ctx)

    return _final_norm(xs, g_final, n_ctx)[None]
```

```python
import functools

import numpy as np
import jax
import jax.numpy as jnp
from jax import lax
from jax.experimental import pallas as pl
from jax.experimental.pallas import tpu as pltpu

F32 = jnp.float32
BF16 = jnp.bfloat16

D_MODEL = 2048
DEPTH = 2
GRID_W = 64
HEAD_DIM = 64
N_BRANCH = 4
BRANCH_W = D_MODEL // N_BRANCH
WIN_ROWS = 8
WIN_COLS = 16
RG_C = 8.0
RG_CONV = 4
CONF_WIDTH = 31
GQA_KV_HEADS = 2
ROPE_THETA = 10000.0
D_FF = 5632
EPS = 1e-6
N_MOD = 6
N_GATE = N_BRANCH * D_MODEL
N_REST = 3 * BRANCH_W + 2 * BRANCH_W + 2 * BRANCH_W + BRANCH_W + 2 * GQA_KV_HEADS * HEAD_DIM
N_IN = N_GATE + N_REST

ROW_TILE = 256
HALO = 16
NA_ROWS = ROW_TILE // GRID_W
NA_KEY_ROWS = 3 * NA_ROWS
NEG = -1e30
VMEM_LIMIT = 56 * 1024 * 1024

COL_QA, COL_KA, COL_VA, COL_XR, COL_GR, COL_CV, COL_CG, COL_QD = (N_GATE // BRANCH_W + n for n in range(8))
COL_KD = (N_GATE + 8 * BRANCH_W) // 128
COL_VD = COL_KD + 1


def _params(*sem):
    return pltpu.CompilerParams(dimension_semantics=sem, vmem_limit_bytes=VMEM_LIMIT)


def _pick(total, candidates):
    for cand in candidates:
        if total % cand == 0:
            return cand
    raise ValueError(f"no tile for {total} in {candidates}")


def _dot(a, b):
    return jnp.dot(a, b, preferred_element_type=F32)


def _dot_nt(a, b):
    return lax.dot_general(a, b, (((1,), (1,)), ((), ())), preferred_element_type=F32)


def _sigmoid(x):
    return jax.nn.sigmoid(x)


def _ada_kernel(cc_ref, w_ref, b_ref, o_ref):
    a = cc_ref[...]
    a = a * _sigmoid(a)
    o_ref[0] = _dot(a.astype(BF16), w_ref[0].astype(BF16)) + b_ref[0]


def _ada(cc, w_ada, b_ada):
    depth, d, n = w_ada.shape
    tn = 1024
    return pl.pallas_call(
        _ada_kernel,
        grid=(depth, n // tn),
        in_specs=[pl.BlockSpec((8, d), lambda l, j: (0, 0)),
                  pl.BlockSpec((1, d, tn), lambda l, j: (l, 0, j)),
                  pl.BlockSpec((1, 1, tn), lambda l, j: (l, 0, j))],
        out_specs=pl.BlockSpec((1, 8, tn), lambda l, j: (l, 0, j)),
        out_shape=jax.ShapeDtypeStruct((depth, 8, n), F32),
        compiler_params=_params("parallel", "parallel"),
        name="ada",
    )(cc, w_ada, b_ada.reshape(depth, 1, n))


def _cast_kernel(w_ref, o_ref):
    o_ref[...] = w_ref[...].astype(o_ref.dtype)


def _w_in_gates_first(w_in):
    depth, d, n = w_in.shape
    tn = 256
    nb = n // tn
    shift = N_REST // tn
    assert n % tn == 0 and N_REST % tn == 0
    return pl.pallas_call(
        _cast_kernel,
        grid=(depth, nb),
        in_specs=[pl.BlockSpec((None, d, tn), lambda l, j: (l, 0, (j + shift) % nb))],
        out_specs=pl.BlockSpec((None, d, tn), lambda l, j: (l, 0, j)),
        out_shape=jax.ShapeDtypeStruct((depth, d, n), BF16),
        compiler_params=_params("parallel", "parallel"),
        name="w_in_cast",
    )(w_in)


NORM_ROWS = 64


def _norm_modulate(x_ref, h_ref, h_row0, g_ref, mod_ref, shift_row, first_row, n_ctx, keep=None):
    nrows = x_ref.shape[0]
    chunk = min(NORM_ROWS, nrows)
    gain = [g_ref[...] * (1.0 + mod_ref[0, grp, shift_row + 1:shift_row + 2, :]) for grp in range(2)]
    shift = [mod_ref[0, grp, shift_row:shift_row + 1, :] for grp in range(2)]
    if keep is not None:
        gain = [v * keep for v in gain]
        shift = [v * keep for v in shift]

    def body(r, carry):
        r0 = pl.multiple_of(r * chunk, chunk)
        x = x_ref[pl.ds(r0, chunk), :]
        y = x * lax.rsqrt(jnp.mean(x * x, axis=-1, keepdims=True) + EPS)
        is_ctx = first_row + r0 + lax.broadcasted_iota(jnp.int32, (chunk, 1), 0) < n_ctx
        h = y * jnp.where(is_ctx, gain[1], gain[0]) + jnp.where(is_ctx, shift[1], shift[0])
        h_ref[pl.ds(h_row0 + r0, chunk), :] = h.astype(BF16)
        return carry

    lax.fori_loop(0, nrows // chunk, body, 0)


def _proj_kernel(x_ref, g_ref, mod_ref, w_ref, o_ref, h_ref, *, tm, n_ctx, shift_row):
    @pl.when(pl.program_id(1) == 0)
    def _():
        _norm_modulate(x_ref, h_ref, 0, g_ref, mod_ref, shift_row, pl.program_id(0) * tm, n_ctx)

    o_ref[...] = _dot(h_ref[...], w_ref[...]).astype(o_ref.dtype)


def _proj(x, g, mod, layer, shift_row, w, n_ctx, tn_candidates):
    t, d = x.shape
    n = w.shape[2]
    tm = _pick(t, (768, 256))
    tn = _pick(n, tn_candidates)
    return pl.pallas_call(
        functools.partial(_proj_kernel, tm=tm, n_ctx=n_ctx, shift_row=shift_row),
        grid=(t // tm, n // tn),
        in_specs=[pl.BlockSpec((tm, d), lambda i, j: (i, 0)),
                  pl.BlockSpec((1, d), lambda i, j: (0, 0)),
                  pl.BlockSpec((1, 2, 8, d), lambda i, j: (layer, 0, 0, 0)),
                  pl.BlockSpec((None, d, tn), lambda i, j: (layer, 0, j))],
        out_specs=pl.BlockSpec((tm, tn), lambda i, j: (i, j)),
        out_shape=jax.ShapeDtypeStruct((t, n), BF16),
        scratch_shapes=[pltpu.VMEM((tm, d), BF16)],
        compiler_params=_params("parallel", "arbitrary"),
        name="proj",
    )(x, g.reshape(1, d), mod, w)


def _softplus(x):
    return jnp.maximum(x, 0.0) + jnp.log(1.0 + jnp.exp(-jnp.abs(x)))


N_SEG = 8


def _scan_rows(a_ref, b_ref, h_ref, carry_ref, reverse):
    nblk, _, width = a_ref.shape
    seg_len = h_ref.shape[0] // N_SEG
    pitch = _seg_pitch(seg_len)
    for c in range(nblk):
        cols = slice(c * width, (c + 1) * width)
        h = jnp.zeros((N_SEG, width), F32)
        p = jnp.ones((N_SEG, width), F32)
        for j in (reversed(range(seg_len)) if reverse else range(seg_len)):
            rows = pl.ds(j, N_SEG, stride=pitch)
            aj = a_ref[c, rows, :]
            h = aj * h + b_ref[c, rows, :]
            p = aj * p
            b_ref[c, rows, :] = h
            a_ref[c, rows, :] = p
        state = carry_ref[0:1, cols]
        for k in (reversed(range(N_SEG)) if reverse else range(N_SEG)):
            h_ref[k * seg_len:(k + 1) * seg_len, cols] = (
                b_ref[c, k * pitch:k * pitch + seg_len, :] + a_ref[c, k * pitch:k * pitch + seg_len, :] * state)
            state = h[k:k + 1, :] + p[k:k + 1, :] * state
        carry_ref[:, cols] = jnp.broadcast_to(state, (carry_ref.shape[0], width))


def _seg_pitch(seg_len):
    return seg_len + 8 if (seg_len // 8) % 2 == 0 else seg_len


def _rg_kernel(xfp, xfm, xfn, xbp, xbm, xbn, cw_ref, wbd_ref, brg_ref, lam_ref, hf_ref, hb_ref,
               ext, a_ref, b_ref, carry_f, carry_b, *, nchunks):
    s = pl.program_id(0)
    tm = xfm.shape[0]
    bw = xfm.shape[1]

    @pl.when(s == 0)
    def _():
        carry_f[...] = jnp.zeros_like(carry_f)
        carry_b[...] = jnp.zeros_like(carry_b)

    cb = jnp.where(s == 0, 0, nchunks - s)

    def gates(d, xp, xm, xn, cidx):
        prev_ok = (cidx >= 2).astype(F32)
        next_ok = jnp.logical_and(cidx >= 1, cidx <= nchunks - 2).astype(F32)
        ext[0:HALO, :] = xp[...].astype(F32) * prev_ok
        ext[HALO:HALO + tm, :] = xm[...].astype(F32)
        ext[HALO + tm:2 * HALO + tm, :] = xn[...].astype(F32) * next_ok
        left = RG_CONV // 2
        xl = cw_ref[0:1, :] * ext[HALO - left:HALO - left + tm, :]
        for k in range(1, RG_CONV):
            xl = xl + cw_ref[k:k + 1, :] * ext[HALO - left + k:HALO - left + k + tm, :]
        g = _sigmoid(_dot(xl.astype(BF16), wbd_ref[d]) + brg_ref[d])
        log_a = (-RG_C * g[:, :bw]) * _softplus(-lam_ref[d])
        a = jnp.exp(log_a)
        b = jnp.sqrt(-jnp.tanh(log_a) * (a * a + 1.0)) * g[:, bw:] * xl
        seg_len = tm // N_SEG
        pitch = _seg_pitch(seg_len)
        for c in range(bw // 128):
            for k in range(N_SEG):
                a_ref[d, c, k * pitch:k * pitch + seg_len, :] = a[k * seg_len:(k + 1) * seg_len, 128 * c:128 * (c + 1)]
                b_ref[d, c, k * pitch:k * pitch + seg_len, :] = b[k * seg_len:(k + 1) * seg_len, 128 * c:128 * (c + 1)]

    gates(0, xfp, xfm, xfn, s)
    gates(1, xbp, xbm, xbn, cb)
    _scan_rows(a_ref.at[0], b_ref.at[0], hf_ref, carry_f, reverse=False)
    _scan_rows(a_ref.at[1], b_ref.at[1], hb_ref, carry_b, reverse=True)


def _rglru(z, rg_conv, wbd, brg, lam):
    t = z.shape[0]
    tm = ROW_TILE
    nchunks = t // tm
    hpt = tm // HALO
    nh = t // HALO

    def bwd(s):
        return jnp.where(s == 0, 0, nchunks - s)

    def prev_map(cidx):
        return lambda s: (jnp.maximum(cidx(s) * hpt - 1, 0), COL_XR)

    def next_map(cidx):
        return lambda s: (jnp.minimum((cidx(s) + 1) * hpt, nh - 1), COL_XR)

    fwd = lambda s: s
    in_specs = [
        pl.BlockSpec((HALO, BRANCH_W), prev_map(fwd)),
        pl.BlockSpec((tm, BRANCH_W), lambda s: (s, COL_XR)),
        pl.BlockSpec((HALO, BRANCH_W), next_map(fwd)),
        pl.BlockSpec((HALO, BRANCH_W), prev_map(bwd)),
        pl.BlockSpec((tm, BRANCH_W), lambda s: (bwd(s), COL_XR)),
        pl.BlockSpec((HALO, BRANCH_W), next_map(bwd)),
        pl.BlockSpec((RG_CONV, BRANCH_W), lambda s: (0, 0)),
        pl.BlockSpec((2, BRANCH_W, 2 * BRANCH_W), lambda s: (0, 0, 0)),
        pl.BlockSpec((2, 1, 2 * BRANCH_W), lambda s: (0, 0, 0)),
        pl.BlockSpec((2, 1, BRANCH_W), lambda s: (0, 0, 0)),
    ]
    return pl.pallas_call(
        functools.partial(_rg_kernel, nchunks=nchunks),
        grid=(nchunks,),
        in_specs=in_specs,
        out_specs=[pl.BlockSpec((tm, BRANCH_W), lambda s: (s, 0)),
                   pl.BlockSpec((tm, BRANCH_W), lambda s: (bwd(s), 0))],
        out_shape=[jax.ShapeDtypeStruct((t, BRANCH_W), F32)] * 2,
        scratch_shapes=[pltpu.VMEM((tm + 2 * HALO, BRANCH_W), F32),
                        pltpu.VMEM((2, BRANCH_W // 128, N_SEG * _seg_pitch(tm // N_SEG), 128), F32),
                        pltpu.VMEM((2, BRANCH_W // 128, N_SEG * _seg_pitch(tm // N_SEG), 128), F32),
                        pltpu.VMEM((8, BRANCH_W), F32),
                        pltpu.VMEM((8, BRANCH_W), F32)],
        compiler_params=_params("arbitrary"),
        name="rglru",
    )(z, z, z, z, z, z, rg_conv, wbd, brg, lam)


def _conf_kernel(vp, vm, vn, gp, gm, gn, dw_ref, lng_ref, lnb_ref, o_ref, ext, shifted, *, nchunks):
    i = pl.program_id(0)
    tm = vm.shape[0]
    prev_ok = (i >= 2).astype(F32)
    next_ok = jnp.logical_and(i >= 1, i <= nchunks - 2).astype(F32)

    def glu(v, g):
        return v[...].astype(F32) * _sigmoid(g[...].astype(F32))

    ext[0:HALO, :] = glu(vp, gp) * prev_ok
    ext[HALO:HALO + tm, :] = glu(vm, gm)
    ext[HALO + tm:2 * HALO + tm, :] = glu(vn, gn) * next_ok
    base = HALO - CONF_WIDTH // 2
    span = tm + 2 * HALO - 8
    for r in range(1, 8):
        shifted[r - 1] = ext[r:r + span, :]
    acc = None
    for k in range(CONF_WIDTH):
        q, r = divmod(base + k, 8)
        rows = ext[8 * q:8 * q + tm, :] if r == 0 else shifted[r - 1, 8 * q:8 * q + tm, :]
        term = dw_ref[k:k + 1, :] * rows
        acc = term if acc is None else acc + term
    mu = jnp.mean(acc, axis=-1, keepdims=True)
    xc = acc - mu
    y = xc * lax.rsqrt(jnp.mean(xc * xc, axis=-1, keepdims=True) + EPS) * lng_ref[...] + lnb_ref[...]
    o_ref[...] = (y * _sigmoid(y)).astype(o_ref.dtype)


def _conformer(z, conf_dw, ln_g, ln_b):
    t = z.shape[0]
    tm = ROW_TILE
    nchunks = t // tm
    hpt = tm // HALO
    nh = t // HALO

    def specs(col):
        return [pl.BlockSpec((HALO, BRANCH_W), lambda i: (jnp.maximum(i * hpt - 1, 0), col)),
                pl.BlockSpec((tm, BRANCH_W), lambda i: (i, col)),
                pl.BlockSpec((HALO, BRANCH_W), lambda i: (jnp.minimum((i + 1) * hpt, nh - 1), col))]

    return pl.pallas_call(
        functools.partial(_conf_kernel, nchunks=nchunks),
        grid=(nchunks,),
        in_specs=specs(COL_CV) + specs(COL_CG) + [
            pl.BlockSpec((CONF_WIDTH, BRANCH_W), lambda i: (0, 0)),
            pl.BlockSpec((1, BRANCH_W), lambda i: (0, 0)),
            pl.BlockSpec((1, BRANCH_W), lambda i: (0, 0))],
        out_specs=pl.BlockSpec((tm, BRANCH_W), lambda i: (i, 0)),
        out_shape=jax.ShapeDtypeStruct((t, BRANCH_W), BF16),
        scratch_shapes=[pltpu.VMEM((tm + 2 * HALO, BRANCH_W), F32),
                        pltpu.VMEM((7, tm + 2 * HALO - 8, BRANCH_W), F32)],
        compiler_params=_params("parallel"),
        name="conformer",
    )(z, z, z, z, z, z, conf_dw, ln_g.reshape(1, -1), ln_b.reshape(1, -1))


def _na_bias_table(rpb):
    i = np.arange(NA_ROWS)[:, None]
    j = np.arange(NA_KEY_ROWS)[None, :]
    variants = [(np.zeros_like(i), i), (i, NA_ROWS + i), (np.full_like(i, NA_ROWS), 2 * NA_ROWS + i)]
    c = np.arange(GRID_W)[:, None]
    kc = np.arange(GRID_W)[None, :]
    col_start = np.clip(c - WIN_COLS // 2, 0, GRID_W - WIN_COLS)
    col_ok = (kc >= col_start) & (kc < col_start + WIN_COLS)
    nh = rpb.shape[0]
    padded = jnp.pad(rpb.astype(F32), ((0, 0), (0, 0), (GRID_W, GRID_W)))
    off = GRID_W + WIN_COLS - 1
    toeplitz = jnp.stack([padded[:, :, off - cc:off - cc + GRID_W] for cc in range(GRID_W)], axis=2)
    toeplitz = jnp.where(col_ok[None, None], toeplitz, NEG)
    masked = jnp.full((nh, GRID_W, GRID_W), NEG, F32)
    tables = [jnp.full((nh, ROW_TILE, NA_KEY_ROWS * GRID_W), NEG, F32)]
    for start, qrow in variants:
        row_ok = (j >= start) & (j < start + WIN_ROWS)
        dr = j - qrow + WIN_ROWS - 1
        rows = [jnp.concatenate([toeplitz[:, dr[qi, kj]] if row_ok[qi, kj] else masked for kj in range(NA_KEY_ROWS)],
                                axis=-1) for qi in range(NA_ROWS)]
        tables.append(jnp.stack(rows, axis=1).reshape(nh, ROW_TILE, NA_KEY_ROWS * GRID_W))
    return jnp.stack(tables)


def _na_kernel(q_ref, k0, k1, k2, kc, v0, v1, v2, vc, bias_ref, o_ref):
    tm = q_ref.shape[0]
    lane = lax.broadcasted_iota(jnp.int32, (1, 128), 1)
    scale = HEAD_DIM ** -0.5
    qmask = [jnp.where(lane < HEAD_DIM, scale, 0.0).astype(BF16), jnp.where(lane >= HEAD_DIM, scale, 0.0).astype(BF16)]
    first_half = lax.broadcasted_iota(jnp.int32, (tm, 128), 1) < HEAD_DIM
    one = jnp.ones((tm, 128), BF16)

    def scores(h):
        cols = slice(128 * (h // 2), 128 * (h // 2 + 1))
        qm = q_ref[:, cols] * qmask[h % 2]
        s = [_dot_nt(qm, r[:, cols]) for r in (k0, k1, k2, kc)]
        return [s[jb] + bias_ref[0, h, :, tm * jb:tm * (jb + 1)] for jb in range(3)] + [s[3]]

    n_heads = BRANCH_W // HEAD_DIM
    s_next = scores(0)
    pair = []
    for h in range(n_heads):
        s = s_next
        if h + 1 < n_heads:
            s_next = scores(h + 1)
        cols = slice(128 * (h // 2), 128 * (h // 2 + 1))
        m = jnp.max(jnp.maximum(jnp.maximum(s[0], s[1]), jnp.maximum(s[2], s[3])), axis=-1, keepdims=True)
        o = None
        for sj, r in zip(s, (v0, v1, v2, vc)):
            va = jnp.where(first_half, r[:, cols], one) if h % 2 == 0 else jnp.where(first_half, one, r[:, cols])
            term = _dot(jnp.exp(sj - m).astype(BF16), va)
            o = term if o is None else o + term
        pair.append(o)
        if h % 2 == 1:
            num = jnp.where(first_half, pair[0], pair[1])
            den = jnp.where(first_half, pltpu.roll(pair[0], HEAD_DIM, 1), pltpu.roll(pair[1], HEAD_DIM, 1))
            o_ref[:, cols] = (num * (1.0 / den)).astype(o_ref.dtype)
            pair = []


def _natten(z, bias):
    t = z.shape[0]
    tm = ROW_TILE
    nt = t // tm
    assert nt >= 4

    def kbase(i):
        return jnp.clip(i - 1, 1, nt - 3)

    def variant(i):
        return jnp.where(i == 0, 0, jnp.where(i == 1, 1, jnp.where(i == nt - 1, 3, 2)))

    def kv_specs(col):
        return [pl.BlockSpec((tm, BRANCH_W), lambda i: (kbase(i), col)),
                pl.BlockSpec((tm, BRANCH_W), lambda i: (kbase(i) + 1, col)),
                pl.BlockSpec((tm, BRANCH_W), lambda i: (kbase(i) + 2, col)),
                pl.BlockSpec((tm, BRANCH_W), lambda i: (0, col))]

    nh = bias.shape[1]
    return pl.pallas_call(
        _na_kernel,
        grid=(nt,),
        in_specs=[pl.BlockSpec((tm, BRANCH_W), lambda i: (i, COL_QA))] + kv_specs(COL_KA) + kv_specs(COL_VA) + [
            pl.BlockSpec((1, nh, tm, 3 * tm), lambda i: (variant(i), 0, 0, 0))],
        out_specs=pl.BlockSpec((tm, BRANCH_W), lambda i: (i, 0)),
        out_shape=jax.ShapeDtypeStruct((t, BRANCH_W), BF16),
        compiler_params=_params("parallel"),
        name="natten",
    )(z, z, z, z, z, z, z, z, z, bias)


def _rope_tables(seq, n_ctx):
    half = HEAD_DIM // 2
    n_f = half // 2
    inv_freq = ROPE_THETA ** (-jnp.arange(n_f, dtype=F32) / n_f)
    pos = jnp.arange(seq)
    ang_r = (pos // GRID_W).astype(F32)[:, None] * inv_freq[None, :]
    ang_c = (pos % GRID_W).astype(F32)[:, None] * inv_freq[None, :]
    cos = jnp.concatenate([jnp.cos(ang_r), jnp.cos(ang_r), jnp.cos(ang_c), jnp.cos(ang_c)], axis=-1)
    sin = jnp.concatenate([-jnp.sin(ang_r), jnp.sin(ang_r), -jnp.sin(ang_c), jnp.sin(ang_c)], axis=-1)
    cos = jnp.concatenate([jnp.ones((n_ctx, HEAD_DIM), F32), cos], axis=0)
    sin = jnp.concatenate([jnp.zeros((n_ctx, HEAD_DIM), F32), sin], axis=0)
    return jnp.tile(cos, (1, 2)), jnp.tile(sin, (1, 2))


def _qkprep_kernel(q_ref, k_ref, v_ref, cos_ref, sin_ref, qg_ref, kg_ref, aq_ref, ak_ref, qo_ref, ko_ref, vo_ref):
    n_f = HEAD_DIM // 4

    def norm_rope(x, g, avg, cos, sin):
        n = x.shape[1]
        x2 = x * x
        hi = x2.astype(BF16)
        lo = (x2 - hi.astype(F32)).astype(BF16)
        ms = _dot(hi, avg) + _dot(lo, avg)
        y = x * lax.rsqrt(ms + EPS) * g
        lane = lax.broadcasted_iota(jnp.int32, x.shape, 1) % (2 * n_f)
        partner = jnp.where(lane < n_f, pltpu.roll(y, n - n_f, 1), pltpu.roll(y, n_f, 1))
        return y * cos + partner * sin

    cos = cos_ref[...]
    sin = sin_ref[...]
    reps = q_ref.shape[1] // 128
    q = norm_rope(q_ref[...].astype(F32), qg_ref[...], aq_ref[...],
                  jnp.concatenate([cos] * reps, axis=1), jnp.concatenate([sin] * reps, axis=1))
    qo_ref[...] = (q * HEAD_DIM ** -0.5).astype(BF16)

    first = lax.broadcasted_iota(jnp.int32, k_ref.shape, 1) < HEAD_DIM
    k = norm_rope(k_ref[...].astype(F32), kg_ref[...], ak_ref[...], cos, sin)
    k_sw = pltpu.roll(k, HEAD_DIM, 1)
    kdup = jnp.concatenate([jnp.where(first, k, k_sw), jnp.where(first, k_sw, k)], axis=1)
    ko_ref[...] = kdup.T.astype(BF16)
    v = v_ref[...].astype(F32)
    v_sw = pltpu.roll(v, HEAD_DIM, 1)
    vo_ref[...] = jnp.concatenate([jnp.where(first, v, 1.0), jnp.where(first, 1.0, v_sw),
                                   jnp.where(first, v_sw, 1.0), jnp.where(first, 1.0, v)], axis=1).astype(BF16)


def _head_avg(n):
    idx = np.arange(n) // HEAD_DIM
    return jnp.asarray((idx[:, None] == idx[None, :]).astype(np.float32) / HEAD_DIM, BF16)


def _qkprep(z, cos, sin, q_norm_g, k_norm_g):
    t = z.shape[0]
    tm = ROW_TILE
    kvw = GQA_KV_HEADS * HEAD_DIM
    qg = jnp.tile(q_norm_g.astype(F32), BRANCH_W // HEAD_DIM).reshape(1, BRANCH_W)
    kg = jnp.tile(k_norm_g.astype(F32), GQA_KV_HEADS).reshape(1, kvw)
    const = lambda i: (0, 0)
    return pl.pallas_call(
        _qkprep_kernel,
        grid=(t // tm,),
        in_specs=[pl.BlockSpec((tm, BRANCH_W), lambda i: (i, COL_QD)),
                  pl.BlockSpec((tm, kvw), lambda i: (i, COL_KD)),
                  pl.BlockSpec((tm, kvw), lambda i: (i, COL_VD)),
                  pl.BlockSpec((tm, kvw), lambda i: (i, 0)),
                  pl.BlockSpec((tm, kvw), lambda i: (i, 0)),
                  pl.BlockSpec((1, BRANCH_W), const),
                  pl.BlockSpec((1, kvw), const),
                  pl.BlockSpec((BRANCH_W, BRANCH_W), const),
                  pl.BlockSpec((kvw, kvw), const)],
        out_specs=[pl.BlockSpec((tm, BRANCH_W), lambda i: (i, 0)),
                   pl.BlockSpec((2 * kvw, tm), lambda i: (0, i)),
                   pl.BlockSpec((tm, 4 * kvw), lambda i: (i, 0))],
        out_shape=[jax.ShapeDtypeStruct((t, BRANCH_W), BF16),
                   jax.ShapeDtypeStruct((2 * kvw, t), BF16),
                   jax.ShapeDtypeStruct((t, 4 * kvw), BF16)],
        compiler_params=_params("parallel"),
        name="qkprep",
    )(z, z, z, cos, sin, qg, kg, _head_avg(BRANCH_W), _head_avg(kvw))


def _gqa_kernel(q_ref, kt_ref, va_ref, o_ref, qs_ref, s_ref, m_ref, acc_ref, *, n_ctx, n_chunks, tk):
    tq = q_ref.shape[0]
    lane = lax.broadcasted_iota(jnp.int32, (1, 128), 1)
    hmask = [(lane < HEAD_DIM).astype(BF16), (lane >= HEAD_DIM).astype(BF16)]
    first_half = lax.broadcasted_iota(jnp.int32, (tq, 128), 1) < HEAD_DIM
    group = BRANCH_W // HEAD_DIM // GQA_KV_HEADS
    for g in range(GQA_KV_HEADS):
        for hh in range(group):
            pair = q_ref[:, 128 * (g * group // 2 + hh // 2):128 * (g * group // 2 + hh // 2 + 1)]
            qs_ref[hh] = pair * hmask[hh % 2]

        def scores(off, size, slot):
            for hh in range(group):
                s_ref[slot, hh, :, 0:size] = _dot(qs_ref[hh], kt_ref[128 * g:128 * (g + 1), pl.ds(off, size)])

        def update(off, size, slot, first):
            for hh in range(group):
                s = s_ref[slot, hh, :, 0:size]
                mx = s[:, 0:128]
                for cc in range(1, size // 128):
                    mx = jnp.maximum(mx, s[:, 128 * cc:128 * (cc + 1)])
                rmax = jnp.max(mx, axis=-1, keepdims=True)
                vcol = 256 * g + 128 * (hh % 2)
                va = va_ref[pl.ds(off, size), vcol:vcol + 128]
                if first:
                    m_new = jnp.broadcast_to(rmax, (tq, 128))
                    p = jnp.exp(s - jnp.concatenate([m_new] * (size // 128), axis=1))
                    acc_ref[hh] = _dot(p.astype(BF16), va)
                else:
                    m_old = m_ref[hh]
                    m_new = jnp.maximum(m_old, rmax)
                    p = jnp.exp(s - jnp.concatenate([m_new] * (size // 128), axis=1))
                    acc_ref[hh] = jnp.exp(m_old - m_new) * acc_ref[hh] + _dot(p.astype(BF16), va)
                m_ref[hh] = m_new

        scores(0, n_ctx, 1)
        if n_chunks:
            scores(n_ctx, tk, 0)
        update(0, n_ctx, 1, True)

        def body(c2, carry):
            off = pl.multiple_of(n_ctx + 2 * c2 * tk, 256)
            scores(off + tk, tk, 1)
            update(off, tk, 0, False)
            scores(off + 2 * tk, tk, 0)
            update(off + tk, tk, 1, False)
            return carry

        if n_chunks:
            lax.fori_loop(0, n_chunks // 2 - 1, body, 0)
            off = n_ctx + (n_chunks - 2) * tk
            scores(off + tk, tk, 1)
            update(off, tk, 0, False)
            update(off + tk, tk, 1, False)

        for pr in range(group // 2):
            o0 = acc_ref[2 * pr]
            o1 = acc_ref[2 * pr + 1]
            num = jnp.where(first_half, o0, o1)
            den = jnp.where(first_half, pltpu.roll(o0, HEAD_DIM, 1), pltpu.roll(o1, HEAD_DIM, 1))
            col = 128 * (g * group // 2 + pr)
            o_ref[:, col:col + 128] = (num * (1.0 / den)).astype(o_ref.dtype)


def _gqa_latent_kernel(q_ref, kt_ref, va_ref, ctx_out_ref, o_ref, *scratch, **kw):
    del ctx_out_ref
    _gqa_kernel(q_ref, kt_ref, va_ref, o_ref, *scratch, **kw)


def _gqa(qn, kt, va, n_ctx):
    t = qn.shape[0]
    seq = t - n_ctx
    tq = _pick(seq, (512, 256))
    tk = _pick(seq, (2048, 1024, 512)) // 2
    assert tk >= n_ctx
    group = BRANCH_W // HEAD_DIM // GQA_KV_HEADS
    resident = [pl.BlockSpec(kt.shape, lambda i: (0, 0), pipeline_mode=pl.Buffered(1)),
                pl.BlockSpec(va.shape, lambda i: (0, 0), pipeline_mode=pl.Buffered(1))]

    def scratch(rows, width):
        return [pltpu.VMEM((group, rows, 128), BF16),
                pltpu.VMEM((2, group, rows, width), F32),
                pltpu.VMEM((group, rows, 128), F32),
                pltpu.VMEM((group, rows, 128), F32)]

    out = pl.pallas_call(
        functools.partial(_gqa_kernel, n_ctx=n_ctx, n_chunks=0, tk=tk),
        grid=(1,),
        in_specs=[pl.BlockSpec((n_ctx, BRANCH_W), lambda i: (0, 0))] + resident,
        out_specs=pl.BlockSpec((n_ctx, BRANCH_W), lambda i: (0, 0)),
        out_shape=jax.ShapeDtypeStruct((t, BRANCH_W), BF16),
        scratch_shapes=scratch(n_ctx, n_ctx),
        compiler_params=_params("arbitrary"),
        name="gqa_ctx",
    )(qn, kt, va)
    tile = (pl.Element(tq), pl.Element(BRANCH_W))
    rows = lambda i: (pl.multiple_of(n_ctx + i * tq, ROW_TILE), 0)
    return pl.pallas_call(
        functools.partial(_gqa_latent_kernel, n_ctx=n_ctx, n_chunks=seq // tk, tk=tk),
        grid=(seq // tq,),
        in_specs=[pl.BlockSpec(tile, rows)] + resident + [pl.BlockSpec(memory_space=pl.ANY)],
        out_specs=pl.BlockSpec(tile, rows),
        out_shape=jax.ShapeDtypeStruct((t, BRANCH_W), BF16),
        scratch_shapes=scratch(tq, tk),
        input_output_aliases={3: 0},
        compiler_params=_params("parallel"),
        name="gqa",
    )(qn, kt, va, out)


def _gelu_tanh(x):
    return 0.5 * x * (1.0 + jnp.tanh(np.sqrt(2.0 / np.pi).astype(np.float32) * (x + 0.044715 * (x * x * x))))


def _merge_kernel(a_ref, hf_ref, hb_ref, gr_ref, c_ref, d_ref, g0, g1, g2, g3, x_ref, mod_ref, wb_ref, wo_ref, o_ref):
    b = ((hf_ref[...] + hb_ref[...]) * _gelu_tanh(gr_ref[...].astype(F32))).astype(BF16)
    branches = (a_ref[...], b, c_ref[...], d_ref[...])
    gates = (g0, g1, g2, g3)
    y = None
    for n in range(N_BRANCH):
        term = _sigmoid(gates[n][...].astype(F32)) * _dot(branches[n], wb_ref[n])
        y = term if y is None else y + term
    out = _dot(y.astype(BF16), wo_ref[...])
    o_ref[...] = x_ref[...] + mod_ref[0, 0, 2:3, :] * out


def _merge(a, hf, hb, z, c, dd, x, mod, layer, wb, wo, n_ctx):
    t, d = x.shape
    tm = ROW_TILE
    nctx_tiles = n_ctx // tm
    row = lambda i: (i, 0)
    single = pl.Buffered(1)
    return pl.pallas_call(
        _merge_kernel,
        grid=(t // tm,),
        in_specs=[pl.BlockSpec((tm, BRANCH_W), row),
                  pl.BlockSpec((tm, BRANCH_W), row),
                  pl.BlockSpec((tm, BRANCH_W), row),
                  pl.BlockSpec((tm, BRANCH_W), lambda i: (i, COL_GR)),
                  pl.BlockSpec((tm, BRANCH_W), row),
                  pl.BlockSpec((tm, BRANCH_W), row)] + [
                      pl.BlockSpec((tm, d), functools.partial(lambda n, i: (i, n), n)) for n in range(N_BRANCH)] + [
                  pl.BlockSpec((tm, d), row),
                  pl.BlockSpec((1, 1, 8, d), lambda i: (layer, (i < nctx_tiles).astype(jnp.int32), 0, 0)),
                  pl.BlockSpec((None, N_BRANCH, BRANCH_W, d), lambda i: (layer, 0, 0, 0), pipeline_mode=single),
                  pl.BlockSpec((None, d, d), lambda i: (layer, 0, 0), pipeline_mode=single)],
        out_specs=pl.BlockSpec((tm, d), row),
        out_shape=jax.ShapeDtypeStruct((t, d), F32),
        compiler_params=_params("parallel"),
        name="merge",
    )(a, hf, hb, z, c, dd, z, z, z, z, x, mod, wb, wo)


def _ffn_up_kernel(xp_ref, x_ref, xn_ref, g_ref, mod_ref, wg_ref, wv_ref, dwg_ref, dwv_ref, o_ref,
                   h_ref, ug_ref, uv_ref, *, tm, n_ctx):
    i = pl.program_id(0)

    @pl.when(pl.program_id(1) == 0)
    def _():
        prev_ok = (i > 0).astype(F32)
        next_ok = (i < pl.num_programs(0) - 1).astype(F32)
        _norm_modulate(xp_ref, h_ref, 0, g_ref, mod_ref, 3, i * tm - HALO, n_ctx, keep=prev_ok)
        _norm_modulate(x_ref, h_ref, HALO, g_ref, mod_ref, 3, i * tm, n_ctx)
        _norm_modulate(xn_ref, h_ref, HALO + tm, g_ref, mod_ref, 3, (i + 1) * tm, n_ctx, keep=next_ok)

    rows = i * tm + lax.broadcasted_iota(jnp.int32, (tm, 1), 0)
    keep_prev = (rows != n_ctx).astype(F32)
    keep_next = (rows != n_ctx - 1).astype(F32)

    def conv(u_ref, w_ref, dw_ref):
        u_ref[...] = _dot(h_ref[...], w_ref[...])
        prev = u_ref[HALO - 1:HALO - 1 + tm, :] * keep_prev
        nxt = u_ref[HALO + 1:HALO + 1 + tm, :] * keep_next
        return dw_ref[0:1, :] * prev + dw_ref[1:2, :] * u_ref[HALO:HALO + tm, :] + dw_ref[2:3, :] * nxt

    ug = conv(ug_ref, wg_ref, dwg_ref)
    uv = conv(uv_ref, wv_ref, dwv_ref)
    o_ref[...] = (ug * _sigmoid(ug) * uv).astype(o_ref.dtype)


def _ffn_up(x, g, mod, layer, w_up, ffn_dw, n_ctx):
    t, d = x.shape
    dff = w_up.shape[2] // 2
    tm = _pick(t, (768, 256))
    tn = _pick(dff, (512, 256))
    nj = dff // tn
    hpt = tm // HALO
    nh = t // HALO
    return pl.pallas_call(
        functools.partial(_ffn_up_kernel, tm=tm, n_ctx=n_ctx),
        grid=(t // tm, nj),
        in_specs=[pl.BlockSpec((HALO, d), lambda i, j: (jnp.maximum(i * hpt - 1, 0), 0)),
                  pl.BlockSpec((tm, d), lambda i, j: (i, 0)),
                  pl.BlockSpec((HALO, d), lambda i, j: (jnp.minimum((i + 1) * hpt, nh - 1), 0)),
                  pl.BlockSpec((1, d), lambda i, j: (0, 0)),
                  pl.BlockSpec((1, 2, 8, d), lambda i, j: (layer, 0, 0, 0)),
                  pl.BlockSpec((None, d, tn), lambda i, j: (layer, 0, j)),
                  pl.BlockSpec((None, d, tn), lambda i, j: (layer, 0, j + nj)),
                  pl.BlockSpec((3, tn), lambda i, j: (0, j)),
                  pl.BlockSpec((3, tn), lambda i, j: (0, j + nj))],
        out_specs=pl.BlockSpec((tm, tn), lambda i, j: (i, j)),
        out_shape=jax.ShapeDtypeStruct((t, dff), BF16),
        scratch_shapes=[pltpu.VMEM((tm + 2 * HALO, d), BF16),
                        pltpu.VMEM((tm + 2 * HALO, tn), F32),
                        pltpu.VMEM((tm + 2 * HALO, tn), F32)],
        compiler_params=_params("parallel", "arbitrary"),
        name="ffn_up",
    )(x, x, x, g.reshape(1, d), mod, w_up, w_up, ffn_dw, ffn_dw)


def _ffn_down_kernel(a_ref, w_ref, x_ref, mod_ref, o_ref, *, tm, n_ctx):
    rows = pl.program_id(0) * tm + lax.broadcasted_iota(jnp.int32, (tm, 1), 0)
    gate = jnp.where(rows < n_ctx, mod_ref[0, 1, 5:6, :], mod_ref[0, 0, 5:6, :])
    o_ref[...] = x_ref[...] + gate * _dot(a_ref[...], w_ref[...])


def _ffn_down(act, wd, x, mod, layer, n_ctx):
    t, d = x.shape
    dff = wd.shape[1]
    tm = _pick(t, (768, 256))
    tn = 512
    return pl.pallas_call(
        functools.partial(_ffn_down_kernel, tm=tm, n_ctx=n_ctx),
        grid=(t // tm, d // tn),
        in_specs=[pl.BlockSpec((tm, dff), lambda i, j: (i, 0)),
                  pl.BlockSpec((None, dff, tn), lambda i, j: (layer, 0, j)),
                  pl.BlockSpec((tm, tn), lambda i, j: (i, j)),
                  pl.BlockSpec((1, 2, 8, tn), lambda i, j: (layer, 0, 0, j))],
        out_specs=pl.BlockSpec((tm, tn), lambda i, j: (i, j)),
        out_shape=jax.ShapeDtypeStruct((t, d), F32),
        compiler_params=_params("parallel", "arbitrary"),
        name="ffn_down",
    )(act, wd, x, mod)


def _final_kernel(x_ref, g_ref, o_ref):
    x = x_ref[...]
    o_ref[...] = x * lax.rsqrt(jnp.mean(x * x, axis=-1, keepdims=True) + EPS) * g_ref[...]


def _final_norm(x, g, n_ctx):
    t, d = x.shape
    tm = ROW_TILE
    skip = n_ctx // tm
    return pl.pallas_call(
        _final_kernel,
        grid=((t - n_ctx) // tm,),
        in_specs=[pl.BlockSpec((tm, d), lambda i: (i + skip, 0)),
                  pl.BlockSpec((1, d), lambda i: (0, 0))],
        out_specs=pl.BlockSpec((tm, d), lambda i: (i, 0)),
        out_shape=jax.ShapeDtypeStruct((t - n_ctx, d), F32),
        compiler_params=_params("parallel"),
        name="final_norm",
    )(x, g.reshape(1, d))


def _block_diag(w):
    nb, bi, bj = w.shape
    eye = jnp.eye(nb, dtype=w.dtype)
    return jnp.einsum('nij,nm->nimj', w, eye).reshape(nb * bi, nb * bj)


def kernel(x, c, ctx, c_ctx, w_ada, b_ada, g_mix, w_in, na_rpb, rg_conv, w_rg, b_rg, rg_lambda, conf_dw, conf_ln_g,
           conf_ln_b, q_norm_g, k_norm_g, w_branch, w_out, g_ffn, w_up, ffn_dw, w_down, g_final):
    bsz, seq, d = x.shape
    n_ctx = ctx.shape[1]
    assert bsz == 1 and d == D_MODEL and n_ctx == ROW_TILE and seq % ROW_TILE == 0
    assert seq // GRID_W >= NA_KEY_ROWS

    cc = jnp.concatenate([c, c_ctx[None, :], jnp.zeros((6, d), F32)], axis=0)
    mod = _ada(cc, w_ada, b_ada)[:, :2, :].reshape(DEPTH, 2, N_MOD, d)
    mod = jnp.pad(mod, ((0, 0), (0, 0), (0, 8 - N_MOD), (0, 0)))

    xs = jnp.concatenate([ctx[0], x[0]], axis=0)
    cos, sin = _rope_tables(seq, n_ctx)

    w_in_b = _w_in_gates_first(w_in)
    w_branch_b, w_out_b, w_up_b, w_down_b = (w.astype(BF16) for w in (w_branch, w_out, w_up, w_down))

    for l in range(DEPTH):
        z = _proj(xs, g_mix[l], mod, l, 0, w_in_b, n_ctx, (1792, 896, 256))

        wbd = jnp.stack([jnp.concatenate([_block_diag(w_rg[l, dd, 0]), _block_diag(w_rg[l, dd, 1])], axis=1)
                         for dd in range(2)]).astype(BF16)
        brg = b_rg[l].reshape(2, 1, 2 * BRANCH_W)
        hf, hb = _rglru(z, rg_conv[l], wbd, brg, rg_lambda[l].reshape(2, 1, BRANCH_W))
        c_br = _conformer(z, conf_dw[l], conf_ln_g[l], conf_ln_b[l])
        a_br = _natten(z, _na_bias_table(na_rpb[l]))
        qn, kt, va = _qkprep(z, cos, sin, q_norm_g[l], k_norm_g[l])
        d_br = _gqa(qn, kt, va, n_ctx)
        xs = _merge(a_br, hf, hb, z, c_br, d_br, xs, mod, l, w_branch_b, w_out_b, n_ctx)

        act = _ffn_up(xs, g_ffn[l], mod, l, w_up_b, ffn_dw[l], n_ctx)
        xs = _ffn_down(act, w_down_b, xs, mod, l, n_ctx)

    return _final_norm(xs, g_final, n_ctx)[None]
```

```python
import functools

import numpy as np
import jax
import jax.numpy as jnp
from jax import lax
from jax.experimental import pallas as pl
from jax.experimental.pallas import tpu as pltpu

F32 = jnp.float32
BF16 = jnp.bfloat16

D_MODEL = 2048
DEPTH = 2
GRID_W = 64
HEAD_DIM = 64
N_BRANCH = 4
BRANCH_W = D_MODEL // N_BRANCH
WIN_ROWS = 8
WIN_COLS = 16
RG_C = 8.0
RG_CONV = 4
CONF_WIDTH = 31
GQA_KV_HEADS = 2
ROPE_THETA = 10000.0
D_FF = 5632
EPS = 1e-6
N_MOD = 6
N_GATE = N_BRANCH * D_MODEL
N_REST = 3 * BRANCH_W + 2 * BRANCH_W + 2 * BRANCH_W + BRANCH_W + 2 * GQA_KV_HEADS * HEAD_DIM
N_IN = N_GATE + N_REST

ROW_TILE = 256
HALO = 16
NA_ROWS = ROW_TILE // GRID_W
NA_KEY_ROWS = 3 * NA_ROWS
NEG = -1e30
VMEM_LIMIT = 56 * 1024 * 1024

COL_QA, COL_KA, COL_VA, COL_XR, COL_GR, COL_CV, COL_CG, COL_QD = (N_GATE // BRANCH_W + n for n in range(8))
COL_KD = (N_GATE + 8 * BRANCH_W) // 128
COL_VD = COL_KD + 1


def _params(*sem):
    return pltpu.CompilerParams(dimension_semantics=sem, vmem_limit_bytes=VMEM_LIMIT)


def _pick(total, candidates):
    for cand in candidates:
        if total % cand == 0:
            return cand
    raise ValueError(f"no tile for {total} in {candidates}")


def _dot(a, b):
    return jnp.dot(a, b, preferred_element_type=F32)


def _dot_nt(a, b):
    return lax.dot_general(a, b, (((1,), (1,)), ((), ())), preferred_element_type=F32)


def _sigmoid(x):
    return jax.nn.sigmoid(x)


def _ada_kernel(cc_ref, w_ref, b_ref, o_ref):
    a = cc_ref[...]
    a = a * _sigmoid(a)
    o_ref[0] = _dot(a.astype(BF16), w_ref[0].astype(BF16)) + b_ref[0]


def _ada(cc, w_ada, b_ada):
    depth, d, n = w_ada.shape
    tn = 1024
    return pl.pallas_call(
        _ada_kernel,
        grid=(depth, n // tn),
        in_specs=[pl.BlockSpec((8, d), lambda l, j: (0, 0)),
                  pl.BlockSpec((1, d, tn), lambda l, j: (l, 0, j)),
                  pl.BlockSpec((1, 1, tn), lambda l, j: (l, 0, j))],
        out_specs=pl.BlockSpec((1, 8, tn), lambda l, j: (l, 0, j)),
        out_shape=jax.ShapeDtypeStruct((depth, 8, n), F32),
        compiler_params=_params("parallel", "parallel"),
        name="ada",
    )(cc, w_ada, b_ada.reshape(depth, 1, n))


def _cast_kernel(w_ref, o_ref):
    o_ref[...] = w_ref[...].astype(o_ref.dtype)


def _w_in_gates_first(w_in):
    depth, d, n = w_in.shape
    tn = 256
    nb = n // tn
    shift = N_REST // tn
    assert n % tn == 0 and N_REST % tn == 0
    return pl.pallas_call(
        _cast_kernel,
        grid=(nb,),
        in_specs=[pl.BlockSpec((depth, d, tn), lambda j: (0, 0, (j + shift) % nb))],
        out_specs=pl.BlockSpec((depth, d, tn), lambda j: (0, 0, j)),
        out_shape=jax.ShapeDtypeStruct((depth, d, n), BF16),
        compiler_params=_params("parallel"),
        name="w_in_cast",
    )(w_in)


NORM_ROWS = 64


def _norm_modulate(x_ref, h_ref, h_row0, g_ref, mod_ref, shift_row, first_row, n_ctx, keep=None):
    nrows = x_ref.shape[0]
    chunk = min(NORM_ROWS, nrows)
    gain, shift = _mod_vectors(g_ref, mod_ref, shift_row)
    if keep is not None:
        gain = [v * keep for v in gain]
        shift = [v * keep for v in shift]

    def body(r, carry):
        r0 = pl.multiple_of(r * chunk, chunk)
        h_ref[pl.ds(h_row0 + r0, chunk), :] = _norm_rows(x_ref[pl.ds(r0, chunk), :], gain, shift, first_row + r0, n_ctx)
        return carry

    lax.fori_loop(0, nrows // chunk, body, 0)


def _mod_vectors(g_ref, mod_ref, shift_row):
    gain = [g_ref[...] * (1.0 + mod_ref[0, grp, shift_row + 1:shift_row + 2, :]) for grp in range(2)]
    shift = [mod_ref[0, grp, shift_row:shift_row + 1, :] for grp in range(2)]
    return gain, shift


def _norm_rows(x, gain, shift, first_row, n_ctx):
    y = x * lax.rsqrt(jnp.mean(x * x, axis=-1, keepdims=True) + EPS)
    is_ctx = first_row + lax.broadcasted_iota(jnp.int32, (x.shape[0], 1), 0) < n_ctx
    return (y * jnp.where(is_ctx, gain[1], gain[0]) + jnp.where(is_ctx, shift[1], shift[0])).astype(BF16)


def _proj_kernel(x0_ref, xn_ref, g_ref, mod_ref, w_ref, o_ref, h_ref, *, tm, n_ctx, shift_row, ahead):
    i = pl.program_id(0)
    j = pl.program_id(1)

    @pl.when(jnp.logical_and(i == 0, j == 0))
    def _():
        _norm_modulate(x0_ref, h_ref.at[0], 0, g_ref, mod_ref, shift_row, 0, n_ctx)

    gain, shift = _mod_vectors(g_ref, mod_ref, shift_row)
    for parity in range(2):
        @pl.when(lax.rem(i, 2) == parity)
        def _():
            o_ref[...] = _dot(h_ref[parity], w_ref[...]).astype(o_ref.dtype)
            r0 = pl.multiple_of(jnp.minimum(j, tm // ahead - 1) * ahead, ahead)
            for sub in range(ahead // NORM_ROWS):
                rr = r0 + sub * NORM_ROWS
                h_ref[1 - parity, pl.ds(rr, NORM_ROWS), :] = _norm_rows(
                    xn_ref[pl.ds(rr, NORM_ROWS), :], gain, shift, (i + 1) * tm + rr, n_ctx)


def _proj(x, g, mod, layer, shift_row, w, n_ctx, tn_candidates):
    t, d = x.shape
    n = w.shape[2]
    tm = _pick(t, (768, 256))
    tn = _pick(n, tn_candidates)
    nt = t // tm
    ahead = 2 * NORM_ROWS
    assert tm % ahead == 0 and n // tn >= tm // ahead
    return pl.pallas_call(
        functools.partial(_proj_kernel, tm=tm, n_ctx=n_ctx, shift_row=shift_row, ahead=ahead),
        grid=(nt, n // tn),
        in_specs=[pl.BlockSpec((tm, d), lambda i, j: (0, 0), pipeline_mode=pl.Buffered(1)),
                  pl.BlockSpec((tm, d), lambda i, j: (jnp.minimum(i + 1, nt - 1), 0)),
                  pl.BlockSpec((1, d), lambda i, j: (0, 0)),
                  pl.BlockSpec((1, 2, 8, d), lambda i, j: (layer, 0, 0, 0)),
                  pl.BlockSpec((None, d, tn), lambda i, j: (layer, 0, j))],
        out_specs=pl.BlockSpec((tm, tn), lambda i, j: (i, j)),
        out_shape=jax.ShapeDtypeStruct((t, n), BF16),
        scratch_shapes=[pltpu.VMEM((2, tm, d), BF16)],
        compiler_params=_params("arbitrary", "arbitrary"),
        name="proj",
    )(x, x, g.reshape(1, d), mod, w)


def _softplus(x):
    return jnp.maximum(x, 0.0) + jnp.log(1.0 + jnp.exp(-jnp.abs(x)))


N_SEG = 8


def _scan_rows(a_ref, b_ref, h_ref, carry_ref, reverse):
    nblk, _, width = a_ref.shape
    seg_len = h_ref.shape[0] // N_SEG
    pitch = _seg_pitch(seg_len)
    for c in range(nblk):
        cols = slice(c * width, (c + 1) * width)
        h = jnp.zeros((N_SEG, width), F32)
        p = jnp.ones((N_SEG, width), F32)
        for j in (reversed(range(seg_len)) if reverse else range(seg_len)):
            rows = pl.ds(j, N_SEG, stride=pitch)
            aj = a_ref[c, rows, :]
            h = aj * h + b_ref[c, rows, :]
            p = aj * p
            b_ref[c, rows, :] = h
            a_ref[c, rows, :] = p
        state = carry_ref[0:1, cols]
        for k in (reversed(range(N_SEG)) if reverse else range(N_SEG)):
            h_ref[k * seg_len:(k + 1) * seg_len, cols] = (
                b_ref[c, k * pitch:k * pitch + seg_len, :] + a_ref[c, k * pitch:k * pitch + seg_len, :] * state)
            state = h[k:k + 1, :] + p[k:k + 1, :] * state
        carry_ref[:, cols] = jnp.broadcast_to(state, (carry_ref.shape[0], width))


def _seg_pitch(seg_len):
    return seg_len + 8 if (seg_len // 8) % 2 == 0 else seg_len


def _rg_kernel(xfp, xfm, xfn, xbp, xbm, xbn, cw_ref, wbd_ref, brg_ref, lam_ref, hf_ref, hb_ref,
               ext, a_ref, b_ref, carry_f, carry_b, *, nchunks):
    s = pl.program_id(0)
    tm = xfm.shape[0]
    bw = xfm.shape[1]

    @pl.when(s == 0)
    def _():
        carry_f[...] = jnp.zeros_like(carry_f)
        carry_b[...] = jnp.zeros_like(carry_b)

    cb = jnp.where(s == 0, 0, nchunks - s)

    def gates(d, xp, xm, xn, cidx):
        prev_ok = (cidx >= 2).astype(F32)
        next_ok = jnp.logical_and(cidx >= 1, cidx <= nchunks - 2).astype(F32)
        ext[0:HALO, :] = xp[...].astype(F32) * prev_ok
        ext[HALO:HALO + tm, :] = xm[...].astype(F32)
        ext[HALO + tm:2 * HALO + tm, :] = xn[...].astype(F32) * next_ok
        left = RG_CONV // 2
        xl = cw_ref[0:1, :] * ext[HALO - left:HALO - left + tm, :]
        for k in range(1, RG_CONV):
            xl = xl + cw_ref[k:k + 1, :] * ext[HALO - left + k:HALO - left + k + tm, :]
        g = _sigmoid(_dot(xl.astype(BF16), wbd_ref[d]) + brg_ref[d])
        log_a = (-RG_C * g[:, :bw]) * _softplus(-lam_ref[d])
        a = jnp.exp(log_a)
        b = jnp.sqrt(-jnp.tanh(log_a) * (a * a + 1.0)) * g[:, bw:] * xl
        seg_len = tm // N_SEG
        pitch = _seg_pitch(seg_len)
        for c in range(bw // 128):
            for k in range(N_SEG):
                a_ref[d, c, k * pitch:k * pitch + seg_len, :] = a[k * seg_len:(k + 1) * seg_len, 128 * c:128 * (c + 1)]
                b_ref[d, c, k * pitch:k * pitch + seg_len, :] = b[k * seg_len:(k + 1) * seg_len, 128 * c:128 * (c + 1)]

    gates(0, xfp, xfm, xfn, s)
    gates(1, xbp, xbm, xbn, cb)
    _scan_rows(a_ref.at[0], b_ref.at[0], hf_ref, carry_f, reverse=False)
    _scan_rows(a_ref.at[1], b_ref.at[1], hb_ref, carry_b, reverse=True)


def _rglru(z, rg_conv, wbd, brg, lam):
    t = z.shape[0]
    tm = ROW_TILE
    nchunks = t // tm
    hpt = tm // HALO
    nh = t // HALO

    def bwd(s):
        return jnp.where(s == 0, 0, nchunks - s)

    def prev_map(cidx):
        return lambda s: (jnp.maximum(cidx(s) * hpt - 1, 0), COL_XR)

    def next_map(cidx):
        return lambda s: (jnp.minimum((cidx(s) + 1) * hpt, nh - 1), COL_XR)

    fwd = lambda s: s
    in_specs = [
        pl.BlockSpec((HALO, BRANCH_W), prev_map(fwd)),
        pl.BlockSpec((tm, BRANCH_W), lambda s: (s, COL_XR)),
        pl.BlockSpec((HALO, BRANCH_W), next_map(fwd)),
        pl.BlockSpec((HALO, BRANCH_W), prev_map(bwd)),
        pl.BlockSpec((tm, BRANCH_W), lambda s: (bwd(s), COL_XR)),
        pl.BlockSpec((HALO, BRANCH_W), next_map(bwd)),
        pl.BlockSpec((RG_CONV, BRANCH_W), lambda s: (0, 0)),
        pl.BlockSpec((2, BRANCH_W, 2 * BRANCH_W), lambda s: (0, 0, 0)),
        pl.BlockSpec((2, 1, 2 * BRANCH_W), lambda s: (0, 0, 0)),
        pl.BlockSpec((2, 1, BRANCH_W), lambda s: (0, 0, 0)),
    ]
    return pl.pallas_call(
        functools.partial(_rg_kernel, nchunks=nchunks),
        grid=(nchunks,),
        in_specs=in_specs,
        out_specs=[pl.BlockSpec((tm, BRANCH_W), lambda s: (s, 0)),
                   pl.BlockSpec((tm, BRANCH_W), lambda s: (bwd(s), 0))],
        out_shape=[jax.ShapeDtypeStruct((t, BRANCH_W), F32)] * 2,
        scratch_shapes=[pltpu.VMEM((tm + 2 * HALO, BRANCH_W), F32),
                        pltpu.VMEM((2, BRANCH_W // 128, N_SEG * _seg_pitch(tm // N_SEG), 128), F32),
                        pltpu.VMEM((2, BRANCH_W // 128, N_SEG * _seg_pitch(tm // N_SEG), 128), F32),
                        pltpu.VMEM((8, BRANCH_W), F32),
                        pltpu.VMEM((8, BRANCH_W), F32)],
        compiler_params=_params("arbitrary"),
        name="rglru",
    )(z, z, z, z, z, z, rg_conv, wbd, brg, lam)


def _conf_kernel(vp, vm, vn, gp, gm, gn, dw_ref, lng_ref, lnb_ref, o_ref, ext, shifted, *, nchunks):
    i = pl.program_id(0)
    tm = vm.shape[0]
    prev_ok = (i >= 2).astype(F32)
    next_ok = jnp.logical_and(i >= 1, i <= nchunks - 2).astype(F32)

    def glu(v, g):
        return v[...].astype(F32) * _sigmoid(g[...].astype(F32))

    ext[0:HALO, :] = glu(vp, gp) * prev_ok
    ext[HALO:HALO + tm, :] = glu(vm, gm)
    ext[HALO + tm:2 * HALO + tm, :] = glu(vn, gn) * next_ok
    base = HALO - CONF_WIDTH // 2
    span = tm + 2 * HALO - 8
    for r in range(1, 8):
        shifted[r - 1] = ext[r:r + span, :]
    acc = None
    for k in range(CONF_WIDTH):
        q, r = divmod(base + k, 8)
        rows = ext[8 * q:8 * q + tm, :] if r == 0 else shifted[r - 1, 8 * q:8 * q + tm, :]
        term = dw_ref[k:k + 1, :] * rows
        acc = term if acc is None else acc + term
    mu = jnp.mean(acc, axis=-1, keepdims=True)
    xc = acc - mu
    y = xc * lax.rsqrt(jnp.mean(xc * xc, axis=-1, keepdims=True) + EPS) * lng_ref[...] + lnb_ref[...]
    o_ref[...] = (y * _sigmoid(y)).astype(o_ref.dtype)


def _conformer(z, conf_dw, ln_g, ln_b):
    t = z.shape[0]
    tm = ROW_TILE
    nchunks = t // tm
    hpt = tm // HALO
    nh = t // HALO

    def specs(col):
        return [pl.BlockSpec((HALO, BRANCH_W), lambda i: (jnp.maximum(i * hpt - 1, 0), col)),
                pl.BlockSpec((tm, BRANCH_W), lambda i: (i, col)),
                pl.BlockSpec((HALO, BRANCH_W), lambda i: (jnp.minimum((i + 1) * hpt, nh - 1), col))]

    return pl.pallas_call(
        functools.partial(_conf_kernel, nchunks=nchunks),
        grid=(nchunks,),
        in_specs=specs(COL_CV) + specs(COL_CG) + [
            pl.BlockSpec((CONF_WIDTH, BRANCH_W), lambda i: (0, 0)),
            pl.BlockSpec((1, BRANCH_W), lambda i: (0, 0)),
            pl.BlockSpec((1, BRANCH_W), lambda i: (0, 0))],
        out_specs=pl.BlockSpec((tm, BRANCH_W), lambda i: (i, 0)),
        out_shape=jax.ShapeDtypeStruct((t, BRANCH_W), BF16),
        scratch_shapes=[pltpu.VMEM((tm + 2 * HALO, BRANCH_W), F32),
                        pltpu.VMEM((7, tm + 2 * HALO - 8, BRANCH_W), F32)],
        compiler_params=_params("parallel"),
        name="conformer",
    )(z, z, z, z, z, z, conf_dw, ln_g.reshape(1, -1), ln_b.reshape(1, -1))


def _na_bias_table(rpb):
    i = np.arange(NA_ROWS)[:, None]
    j = np.arange(NA_KEY_ROWS)[None, :]
    variants = [(np.zeros_like(i), i), (i, NA_ROWS + i), (np.full_like(i, NA_ROWS), 2 * NA_ROWS + i)]
    c = np.arange(GRID_W)[:, None]
    kc = np.arange(GRID_W)[None, :]
    col_start = np.clip(c - WIN_COLS // 2, 0, GRID_W - WIN_COLS)
    col_ok = (kc >= col_start) & (kc < col_start + WIN_COLS)
    nh = rpb.shape[0]
    padded = jnp.pad(rpb.astype(F32), ((0, 0), (0, 0), (GRID_W, GRID_W)))
    off = GRID_W + WIN_COLS - 1
    toeplitz = jnp.stack([padded[:, :, off - cc:off - cc + GRID_W] for cc in range(GRID_W)], axis=2)
    toeplitz = jnp.where(col_ok[None, None], toeplitz, NEG)
    masked = jnp.full((nh, GRID_W, GRID_W), NEG, F32)
    tables = [jnp.full((nh, ROW_TILE, NA_KEY_ROWS * GRID_W), NEG, F32)]
    for start, qrow in variants:
        row_ok = (j >= start) & (j < start + WIN_ROWS)
        dr = j - qrow + WIN_ROWS - 1
        rows = [jnp.concatenate([toeplitz[:, dr[qi, kj]] if row_ok[qi, kj] else masked for kj in range(NA_KEY_ROWS)],
                                axis=-1) for qi in range(NA_ROWS)]
        tables.append(jnp.stack(rows, axis=1).reshape(nh, ROW_TILE, NA_KEY_ROWS * GRID_W))
    return jnp.stack(tables)


def _na_kernel(q_ref, k0, k1, k2, kc, v0, v1, v2, vc, bias_ref, o_ref):
    tm = q_ref.shape[0]
    lane = lax.broadcasted_iota(jnp.int32, (1, 128), 1)
    scale = HEAD_DIM ** -0.5
    qmask = [jnp.where(lane < HEAD_DIM, scale, 0.0).astype(BF16), jnp.where(lane >= HEAD_DIM, scale, 0.0).astype(BF16)]
    first_half = lax.broadcasted_iota(jnp.int32, (tm, 128), 1) < HEAD_DIM
    one = jnp.ones((tm, 128), BF16)

    def scores(h):
        cols = slice(128 * (h // 2), 128 * (h // 2 + 1))
        qm = q_ref[:, cols] * qmask[h % 2]
        s = [_dot_nt(qm, r[:, cols]) for r in (k0, k1, k2, kc)]
        return [s[jb] + bias_ref[0, h, :, tm * jb:tm * (jb + 1)] for jb in range(3)] + [s[3]]

    n_heads = BRANCH_W // HEAD_DIM
    s_next = scores(0)
    pair = []
    for h in range(n_heads):
        s = s_next
        if h + 1 < n_heads:
            s_next = scores(h + 1)
        cols = slice(128 * (h // 2), 128 * (h // 2 + 1))
        m = jnp.max(jnp.maximum(jnp.maximum(s[0], s[1]), jnp.maximum(s[2], s[3])), axis=-1, keepdims=True)
        o = None
        for sj, r in zip(s, (v0, v1, v2, vc)):
            va = jnp.where(first_half, r[:, cols], one) if h % 2 == 0 else jnp.where(first_half, one, r[:, cols])
            term = _dot(jnp.exp(sj - m).astype(BF16), va)
            o = term if o is None else o + term
        pair.append(o)
        if h % 2 == 1:
            num = jnp.where(first_half, pair[0], pair[1])
            den = jnp.where(first_half, pltpu.roll(pair[0], HEAD_DIM, 1), pltpu.roll(pair[1], HEAD_DIM, 1))
            o_ref[:, cols] = (num * (1.0 / den)).astype(o_ref.dtype)
            pair = []


def _natten(z, bias):
    t = z.shape[0]
    tm = ROW_TILE
    nt = t // tm
    assert nt >= 4

    def kbase(i):
        return jnp.clip(i - 1, 1, nt - 3)

    def variant(i):
        return jnp.where(i == 0, 0, jnp.where(i == 1, 1, jnp.where(i == nt - 1, 3, 2)))

    def kv_specs(col):
        return [pl.BlockSpec((tm, BRANCH_W), lambda i: (kbase(i), col)),
                pl.BlockSpec((tm, BRANCH_W), lambda i: (kbase(i) + 1, col)),
                pl.BlockSpec((tm, BRANCH_W), lambda i: (kbase(i) + 2, col)),
                pl.BlockSpec((tm, BRANCH_W), lambda i: (0, col))]

    nh = bias.shape[1]
    return pl.pallas_call(
        _na_kernel,
        grid=(nt,),
        in_specs=[pl.BlockSpec((tm, BRANCH_W), lambda i: (i, COL_QA))] + kv_specs(COL_KA) + kv_specs(COL_VA) + [
            pl.BlockSpec((1, nh, tm, 3 * tm), lambda i: (variant(i), 0, 0, 0))],
        out_specs=pl.BlockSpec((tm, BRANCH_W), lambda i: (i, 0)),
        out_shape=jax.ShapeDtypeStruct((t, BRANCH_W), BF16),
        compiler_params=_params("parallel"),
        name="natten",
    )(z, z, z, z, z, z, z, z, z, bias)


def _rope_tables(seq, n_ctx):
    half = HEAD_DIM // 2
    n_f = half // 2
    inv_freq = ROPE_THETA ** (-jnp.arange(n_f, dtype=F32) / n_f)
    pos = jnp.arange(seq)
    ang_r = (pos // GRID_W).astype(F32)[:, None] * inv_freq[None, :]
    ang_c = (pos % GRID_W).astype(F32)[:, None] * inv_freq[None, :]
    cos = jnp.concatenate([jnp.cos(ang_r), jnp.cos(ang_r), jnp.cos(ang_c), jnp.cos(ang_c)], axis=-1)
    sin = jnp.concatenate([-jnp.sin(ang_r), jnp.sin(ang_r), -jnp.sin(ang_c), jnp.sin(ang_c)], axis=-1)
    cos = jnp.concatenate([jnp.ones((n_ctx, HEAD_DIM), F32), cos], axis=0)
    sin = jnp.concatenate([jnp.zeros((n_ctx, HEAD_DIM), F32), sin], axis=0)
    return jnp.tile(cos, (1, 2)), jnp.tile(sin, (1, 2))


def _qkprep_kernel(q_ref, k_ref, v_ref, cos_ref, sin_ref, qg_ref, kg_ref, aq_ref, ak_ref, qo_ref, ko_ref, vo_ref):
    n_f = HEAD_DIM // 4

    def norm_rope(x, g, avg, cos, sin):
        n = x.shape[1]
        x2 = x * x
        hi = x2.astype(BF16)
        lo = (x2 - hi.astype(F32)).astype(BF16)
        ms = _dot(hi, avg) + _dot(lo, avg)
        y = x * lax.rsqrt(ms + EPS) * g
        lane = lax.broadcasted_iota(jnp.int32, x.shape, 1) % (2 * n_f)
        partner = jnp.where(lane < n_f, pltpu.roll(y, n - n_f, 1), pltpu.roll(y, n_f, 1))
        return y * cos + partner * sin

    cos = cos_ref[...]
    sin = sin_ref[...]
    reps = q_ref.shape[1] // 128
    q = norm_rope(q_ref[...].astype(F32), qg_ref[...], aq_ref[...],
                  jnp.concatenate([cos] * reps, axis=1), jnp.concatenate([sin] * reps, axis=1))
    qo_ref[...] = (q * HEAD_DIM ** -0.5).astype(BF16)

    first = lax.broadcasted_iota(jnp.int32, k_ref.shape, 1) < HEAD_DIM
    k = norm_rope(k_ref[...].astype(F32), kg_ref[...], ak_ref[...], cos, sin)
    k_sw = pltpu.roll(k, HEAD_DIM, 1)
    kdup = jnp.concatenate([jnp.where(first, k, k_sw), jnp.where(first, k_sw, k)], axis=1)
    ko_ref[...] = kdup.T.astype(BF16)
    v = v_ref[...].astype(F32)
    v_sw = pltpu.roll(v, HEAD_DIM, 1)
    vo_ref[...] = jnp.concatenate([jnp.where(first, v, 1.0), jnp.where(first, 1.0, v_sw),
                                   jnp.where(first, v_sw, 1.0), jnp.where(first, 1.0, v)], axis=1).astype(BF16)


def _head_avg(n):
    idx = np.arange(n) // HEAD_DIM
    return jnp.asarray((idx[:, None] == idx[None, :]).astype(np.float32) / HEAD_DIM, BF16)


def _qkprep(z, cos, sin, q_norm_g, k_norm_g):
    t = z.shape[0]
    tm = ROW_TILE
    kvw = GQA_KV_HEADS * HEAD_DIM
    qg = jnp.tile(q_norm_g.astype(F32), BRANCH_W // HEAD_DIM).reshape(1, BRANCH_W)
    kg = jnp.tile(k_norm_g.astype(F32), GQA_KV_HEADS).reshape(1, kvw)
    const = lambda i: (0, 0)
    return pl.pallas_call(
        _qkprep_kernel,
        grid=(t // tm,),
        in_specs=[pl.BlockSpec((tm, BRANCH_W), lambda i: (i, COL_QD)),
                  pl.BlockSpec((tm, kvw), lambda i: (i, COL_KD)),
                  pl.BlockSpec((tm, kvw), lambda i: (i, COL_VD)),
                  pl.BlockSpec((tm, kvw), lambda i: (i, 0)),
                  pl.BlockSpec((tm, kvw), lambda i: (i, 0)),
                  pl.BlockSpec((1, BRANCH_W), const),
                  pl.BlockSpec((1, kvw), const),
                  pl.BlockSpec((BRANCH_W, BRANCH_W), const),
                  pl.BlockSpec((kvw, kvw), const)],
        out_specs=[pl.BlockSpec((tm, BRANCH_W), lambda i: (i, 0)),
                   pl.BlockSpec((2 * kvw, tm), lambda i: (0, i)),
                   pl.BlockSpec((tm, 4 * kvw), lambda i: (i, 0))],
        out_shape=[jax.ShapeDtypeStruct((t, BRANCH_W), BF16),
                   jax.ShapeDtypeStruct((2 * kvw, t), BF16),
                   jax.ShapeDtypeStruct((t, 4 * kvw), BF16)],
        compiler_params=_params("parallel"),
        name="qkprep",
    )(z, z, z, cos, sin, qg, kg, _head_avg(BRANCH_W), _head_avg(kvw))


def _gqa_kernel(q_ref, kt_ref, va_ref, o_ref, qs_ref, s_ref, m_ref, acc_ref, *, n_ctx, n_chunks, tk):
    tq = q_ref.shape[0]
    lane = lax.broadcasted_iota(jnp.int32, (1, 128), 1)
    hmask = [(lane < HEAD_DIM).astype(BF16), (lane >= HEAD_DIM).astype(BF16)]
    first_half = lax.broadcasted_iota(jnp.int32, (tq, 128), 1) < HEAD_DIM
    group = BRANCH_W // HEAD_DIM // GQA_KV_HEADS
    for g in range(GQA_KV_HEADS):
        for hh in range(group):
            pair = q_ref[:, 128 * (g * group // 2 + hh // 2):128 * (g * group // 2 + hh // 2 + 1)]
            qs_ref[hh] = pair * hmask[hh % 2]

        def scores(off, size, slot):
            for hh in range(group):
                s_ref[slot, hh, :, 0:size] = _dot(qs_ref[hh], kt_ref[128 * g:128 * (g + 1), pl.ds(off, size)])

        def update(off, size, slot, first):
            for hh in range(group):
                s = s_ref[slot, hh, :, 0:size]
                mx = s[:, 0:128]
                for cc in range(1, size // 128):
                    mx = jnp.maximum(mx, s[:, 128 * cc:128 * (cc + 1)])
                rmax = jnp.max(mx, axis=-1, keepdims=True)
                vcol = 256 * g + 128 * (hh % 2)
                va = va_ref[pl.ds(off, size), vcol:vcol + 128]
                if first:
                    m_new = jnp.broadcast_to(rmax, (tq, 128))
                    p = jnp.exp(s - jnp.concatenate([m_new] * (size // 128), axis=1))
                    acc_ref[hh] = _dot(p.astype(BF16), va)
                else:
                    m_old = m_ref[hh]
                    m_new = jnp.maximum(m_old, rmax)
                    p = jnp.exp(s - jnp.concatenate([m_new] * (size // 128), axis=1))
                    acc_ref[hh] = jnp.exp(m_old - m_new) * acc_ref[hh] + _dot(p.astype(BF16), va)
                m_ref[hh] = m_new

        scores(0, n_ctx, 1)
        if n_chunks:
            scores(n_ctx, tk, 0)
        update(0, n_ctx, 1, True)

        def body(c2, carry):
            off = pl.multiple_of(n_ctx + 2 * c2 * tk, 256)
            scores(off + tk, tk, 1)
            update(off, tk, 0, False)
            scores(off + 2 * tk, tk, 0)
            update(off + tk, tk, 1, False)
            return carry

        if n_chunks:
            lax.fori_loop(0, n_chunks // 2 - 1, body, 0)
            off = n_ctx + (n_chunks - 2) * tk
            scores(off + tk, tk, 1)
            update(off, tk, 0, False)
            update(off + tk, tk, 1, False)

        for pr in range(group // 2):
            o0 = acc_ref[2 * pr]
            o1 = acc_ref[2 * pr + 1]
            num = jnp.where(first_half, o0, o1)
            den = jnp.where(first_half, pltpu.roll(o0, HEAD_DIM, 1), pltpu.roll(o1, HEAD_DIM, 1))
            col = 128 * (g * group // 2 + pr)
            o_ref[:, col:col + 128] = (num * (1.0 / den)).astype(o_ref.dtype)


def _gqa_latent_kernel(q_ref, kt_ref, va_ref, ctx_out_ref, o_ref, *scratch, **kw):
    del ctx_out_ref
    _gqa_kernel(q_ref, kt_ref, va_ref, o_ref, *scratch, **kw)


def _gqa(qn, kt, va, n_ctx):
    t = qn.shape[0]
    seq = t - n_ctx
    tq = _pick(seq, (512, 256))
    tk = _pick(seq, (2048, 1024, 512)) // 2
    assert tk >= n_ctx
    group = BRANCH_W // HEAD_DIM // GQA_KV_HEADS
    resident = [pl.BlockSpec(kt.shape, lambda i: (0, 0), pipeline_mode=pl.Buffered(1)),
                pl.BlockSpec(va.shape, lambda i: (0, 0), pipeline_mode=pl.Buffered(1))]

    def scratch(rows, width):
        return [pltpu.VMEM((group, rows, 128), BF16),
                pltpu.VMEM((2, group, rows, width), F32),
                pltpu.VMEM((group, rows, 128), F32),
                pltpu.VMEM((group, rows, 128), F32)]

    out = pl.pallas_call(
        functools.partial(_gqa_kernel, n_ctx=n_ctx, n_chunks=0, tk=tk),
        grid=(1,),
        in_specs=[pl.BlockSpec((n_ctx, BRANCH_W), lambda i: (0, 0))] + resident,
        out_specs=pl.BlockSpec((n_ctx, BRANCH_W), lambda i: (0, 0)),
        out_shape=jax.ShapeDtypeStruct((t, BRANCH_W), BF16),
        scratch_shapes=scratch(n_ctx, n_ctx),
        compiler_params=_params("arbitrary"),
        name="gqa_ctx",
    )(qn, kt, va)
    tile = (pl.Element(tq), pl.Element(BRANCH_W))
    rows = lambda i: (pl.multiple_of(n_ctx + i * tq, ROW_TILE), 0)
    return pl.pallas_call(
        functools.partial(_gqa_latent_kernel, n_ctx=n_ctx, n_chunks=seq // tk, tk=tk),
        grid=(seq // tq,),
        in_specs=[pl.BlockSpec(tile, rows)] + resident + [pl.BlockSpec(memory_space=pl.ANY)],
        out_specs=pl.BlockSpec(tile, rows),
        out_shape=jax.ShapeDtypeStruct((t, BRANCH_W), BF16),
        scratch_shapes=scratch(tq, tk),
        input_output_aliases={3: 0},
        compiler_params=_params("parallel"),
        name="gqa",
    )(qn, kt, va, out)


def _gelu_tanh(x):
    return 0.5 * x * (1.0 + jnp.tanh(np.sqrt(2.0 / np.pi).astype(np.float32) * (x + 0.044715 * (x * x * x))))


def _merge_kernel(a_ref, hf_ref, hb_ref, gr_ref, c_ref, d_ref, g0, g1, g2, g3, x_ref, mod_ref, wb_ref, wo_ref, o_ref):
    b = ((hf_ref[...] + hb_ref[...]) * _gelu_tanh(gr_ref[...].astype(F32))).astype(BF16)
    branches = (a_ref[...], b, c_ref[...], d_ref[...])
    gates = (g0, g1, g2, g3)
    y = None
    for n in range(N_BRANCH):
        term = _sigmoid(gates[n][...].astype(F32)) * _dot(branches[n], wb_ref[n])
        y = term if y is None else y + term
    out = _dot(y.astype(BF16), wo_ref[...])
    o_ref[...] = x_ref[...] + mod_ref[0, 0, 2:3, :] * out


def _merge(a, hf, hb, z, c, dd, x, mod, layer, wb, wo, n_ctx):
    t, d = x.shape
    tm = ROW_TILE
    nctx_tiles = n_ctx // tm
    row = lambda i: (i, 0)
    single = pl.Buffered(1)
    return pl.pallas_call(
        _merge_kernel,
        grid=(t // tm,),
        in_specs=[pl.BlockSpec((tm, BRANCH_W), row),
                  pl.BlockSpec((tm, BRANCH_W), row),
                  pl.BlockSpec((tm, BRANCH_W), row),
                  pl.BlockSpec((tm, BRANCH_W), lambda i: (i, COL_GR)),
                  pl.BlockSpec((tm, BRANCH_W), row),
                  pl.BlockSpec((tm, BRANCH_W), row)] + [
                      pl.BlockSpec((tm, d), functools.partial(lambda n, i: (i, n), n)) for n in range(N_BRANCH)] + [
                  pl.BlockSpec((tm, d), row),
                  pl.BlockSpec((1, 1, 8, d), lambda i: (layer, (i < nctx_tiles).astype(jnp.int32), 0, 0)),
                  pl.BlockSpec((None, N_BRANCH, BRANCH_W, d), lambda i: (layer, 0, 0, 0), pipeline_mode=single),
                  pl.BlockSpec((None, d, d), lambda i: (layer, 0, 0), pipeline_mode=single)],
        out_specs=pl.BlockSpec((tm, d), row),
        out_shape=jax.ShapeDtypeStruct((t, d), F32),
        compiler_params=_params("parallel"),
        name="merge",
    )(a, hf, hb, z, c, dd, z, z, z, z, x, mod, wb, wo)


def _ffn_up_kernel(xp_ref, x_ref, xn_ref, g_ref, mod_ref, wg_ref, wv_ref, dwg_ref, dwv_ref, o_ref,
                   h_ref, ug_ref, uv_ref, *, tm, n_ctx):
    i = pl.program_id(0)

    @pl.when(pl.program_id(1) == 0)
    def _():
        prev_ok = (i > 0).astype(F32)
        next_ok = (i < pl.num_programs(0) - 1).astype(F32)
        _norm_modulate(xp_ref, h_ref, 0, g_ref, mod_ref, 3, i * tm - HALO, n_ctx, keep=prev_ok)
        _norm_modulate(x_ref, h_ref, HALO, g_ref, mod_ref, 3, i * tm, n_ctx)
        _norm_modulate(xn_ref, h_ref, HALO + tm, g_ref, mod_ref, 3, (i + 1) * tm, n_ctx, keep=next_ok)

    rows = i * tm + lax.broadcasted_iota(jnp.int32, (tm, 1), 0)
    keep_prev = (rows != n_ctx).astype(F32)
    keep_next = (rows != n_ctx - 1).astype(F32)

    def conv(u_ref, w_ref, dw_ref):
        u_ref[...] = _dot(h_ref[...], w_ref[...])
        prev = u_ref[HALO - 1:HALO - 1 + tm, :] * keep_prev
        nxt = u_ref[HALO + 1:HALO + 1 + tm, :] * keep_next
        return dw_ref[0:1, :] * prev + dw_ref[1:2, :] * u_ref[HALO:HALO + tm, :] + dw_ref[2:3, :] * nxt

    ug = conv(ug_ref, wg_ref, dwg_ref)
    uv = conv(uv_ref, wv_ref, dwv_ref)
    o_ref[...] = (ug * _sigmoid(ug) * uv).astype(o_ref.dtype)


def _ffn_up(x, g, mod, layer, w_up, ffn_dw, n_ctx):
    t, d = x.shape
    dff = w_up.shape[2] // 2
    tm = _pick(t, (768, 256))
    tn = _pick(dff, (512, 256))
    nj = dff // tn
    hpt = tm // HALO
    nh = t // HALO
    return pl.pallas_call(
        functools.partial(_ffn_up_kernel, tm=tm, n_ctx=n_ctx),
        grid=(t // tm, nj),
        in_specs=[pl.BlockSpec((HALO, d), lambda i, j: (jnp.maximum(i * hpt - 1, 0), 0)),
                  pl.BlockSpec((tm, d), lambda i, j: (i, 0)),
                  pl.BlockSpec((HALO, d), lambda i, j: (jnp.minimum((i + 1) * hpt, nh - 1), 0)),
                  pl.BlockSpec((1, d), lambda i, j: (0, 0)),
                  pl.BlockSpec((1, 2, 8, d), lambda i, j: (layer, 0, 0, 0)),
                  pl.BlockSpec((None, d, tn), lambda i, j: (layer, 0, j)),
                  pl.BlockSpec((None, d, tn), lambda i, j: (layer, 0, j + nj)),
                  pl.BlockSpec((3, tn), lambda i, j: (0, j)),
                  pl.BlockSpec((3, tn), lambda i, j: (0, j + nj))],
        out_specs=pl.BlockSpec((tm, tn), lambda i, j: (i, j)),
        out_shape=jax.ShapeDtypeStruct((t, dff), BF16),
        scratch_shapes=[pltpu.VMEM((tm + 2 * HALO, d), BF16),
                        pltpu.VMEM((tm + 2 * HALO, tn), F32),
                        pltpu.VMEM((tm + 2 * HALO, tn), F32)],
        compiler_params=_params("parallel", "arbitrary"),
        name="ffn_up",
    )(x, x, x, g.reshape(1, d), mod, w_up, w_up, ffn_dw, ffn_dw)


def _ffn_down_kernel(a_ref, w_ref, x_ref, mod_ref, o_ref, *, tm, n_ctx):
    rows = pl.program_id(0) * tm + lax.broadcasted_iota(jnp.int32, (tm, 1), 0)
    gate = jnp.where(rows < n_ctx, mod_ref[0, 1, 5:6, :], mod_ref[0, 0, 5:6, :])
    o_ref[...] = x_ref[...] + gate * _dot(a_ref[...], w_ref[...])


def _ffn_down(act, wd, x, mod, layer, n_ctx):
    t, d = x.shape
    dff = wd.shape[1]
    tm = _pick(t, (768, 256))
    tn = 512
    return pl.pallas_call(
        functools.partial(_ffn_down_kernel, tm=tm, n_ctx=n_ctx),
        grid=(t // tm, d // tn),
        in_specs=[pl.BlockSpec((tm, dff), lambda i, j: (i, 0)),
                  pl.BlockSpec((None, dff, tn), lambda i, j: (layer, 0, j)),
                  pl.BlockSpec((tm, tn), lambda i, j: (i, j)),
                  pl.BlockSpec((1, 2, 8, tn), lambda i, j: (layer, 0, 0, j))],
        out_specs=pl.BlockSpec((tm, tn), lambda i, j: (i, j)),
        out_shape=jax.ShapeDtypeStruct((t, d), F32),
        compiler_params=_params("parallel", "arbitrary"),
        name="ffn_down",
    )(act, wd, x, mod)


def _final_kernel(x_ref, g_ref, o_ref):
    x = x_ref[...]
    o_ref[...] = x * lax.rsqrt(jnp.mean(x * x, axis=-1, keepdims=True) + EPS) * g_ref[...]


def _final_norm(x, g, n_ctx):
    t, d = x.shape
    tm = ROW_TILE
    skip = n_ctx // tm
    return pl.pallas_call(
        _final_kernel,
        grid=((t - n_ctx) // tm,),
        in_specs=[pl.BlockSpec((tm, d), lambda i: (i + skip, 0)),
                  pl.BlockSpec((1, d), lambda i: (0, 0))],
        out_specs=pl.BlockSpec((tm, d), lambda i: (i, 0)),
        out_shape=jax.ShapeDtypeStruct((t - n_ctx, d), F32),
        compiler_params=_params("parallel"),
        name="final_norm",
    )(x, g.reshape(1, d))


def _block_diag(w):
    nb, bi, bj = w.shape
    eye = jnp.eye(nb, dtype=w.dtype)
    return jnp.einsum('nij,nm->nimj', w, eye).reshape(nb * bi, nb * bj)


def kernel(x, c, ctx, c_ctx, w_ada, b_ada, g_mix, w_in, na_rpb, rg_conv, w_rg, b_rg, rg_lambda, conf_dw, conf_ln_g,
           conf_ln_b, q_norm_g, k_norm_g, w_branch, w_out, g_ffn, w_up, ffn_dw, w_down, g_final):
    bsz, seq, d = x.shape
    n_ctx = ctx.shape[1]
    assert bsz == 1 and d == D_MODEL and n_ctx == ROW_TILE and seq % ROW_TILE == 0
    assert seq // GRID_W >= NA_KEY_ROWS

    cc = jnp.concatenate([c, c_ctx[None, :], jnp.zeros((6, d), F32)], axis=0)
    mod = _ada(cc, w_ada, b_ada)[:, :2, :].reshape(DEPTH, 2, N_MOD, d)
    mod = jnp.pad(mod, ((0, 0), (0, 0), (0, 8 - N_MOD), (0, 0)))

    xs = jnp.concatenate([ctx[0], x[0]], axis=0)
    cos, sin = _rope_tables(seq, n_ctx)

    w_in_b = _w_in_gates_first(w_in)
    w_branch_b, w_out_b, w_up_b, w_down_b = (w.astype(BF16) for w in (w_branch, w_out, w_up, w_down))

    for l in range(DEPTH):
        z = _proj(xs, g_mix[l], mod, l, 0, w_in_b, n_ctx, (1792, 896, 256))

        wbd = jnp.stack([jnp.concatenate([_block_diag(w_rg[l, dd, 0]), _block_diag(w_rg[l, dd, 1])], axis=1)
                         for dd in range(2)]).astype(BF16)
        brg = b_rg[l].reshape(2, 1, 2 * BRANCH_W)
        hf, hb = _rglru(z, rg_conv[l], wbd, brg, rg_lambda[l].reshape(2, 1, BRANCH_W))
        c_br = _conformer(z, conf_dw[l], conf_ln_g[l], conf_ln_b[l])
        a_br = _natten(z, _na_bias_table(na_rpb[l]))
        qn, kt, va = _qkprep(z, cos, sin, q_norm_g[l], k_norm_g[l])
        d_br = _gqa(qn, kt, va, n_ctx)
        xs = _merge(a_br, hf, hb, z, c_br, d_br, xs, mod, l, w_branch_b, w_out_b, n_ctx)

        act = _ffn_up(xs, g_ffn[l], mod, l, w_up_b, ffn_dw[l], n_ctx)
        xs = _ffn_down(act, w_down_b, xs, mod, l, n_ctx)

    return _final_norm(xs, g_final, n_ctx)[None]
```

```python
import functools

import numpy as np
import jax
import jax.numpy as jnp
from jax import lax
from jax.experimental import pallas as pl
from jax.experimental.pallas import tpu as pltpu

F32 = jnp.float32
BF16 = jnp.bfloat16

D_MODEL = 2048
DEPTH = 2
GRID_W = 64
HEAD_DIM = 64
N_BRANCH = 4
BRANCH_W = D_MODEL // N_BRANCH
WIN_ROWS = 8
WIN_COLS = 16
RG_C = 8.0
RG_CONV = 4
CONF_WIDTH = 31
GQA_KV_HEADS = 2
ROPE_THETA = 10000.0
D_FF = 5632
EPS = 1e-6
N_MOD = 6
N_GATE = N_BRANCH * D_MODEL
N_REST = 3 * BRANCH_W + 2 * BRANCH_W + 2 * BRANCH_W + BRANCH_W + 2 * GQA_KV_HEADS * HEAD_DIM
N_IN = N_GATE + N_REST

ROW_TILE = 256
HALO = 16
NA_ROWS = ROW_TILE // GRID_W
NA_KEY_ROWS = 3 * NA_ROWS
NEG = -1e30
VMEM_LIMIT = 56 * 1024 * 1024

COL_QA, COL_KA, COL_VA, COL_XR, COL_GR, COL_CV, COL_CG, COL_QD = (N_GATE // BRANCH_W + n for n in range(8))
COL_KD = (N_GATE + 8 * BRANCH_W) // 128
COL_VD = COL_KD + 1


def _params(*sem):
    return pltpu.CompilerParams(dimension_semantics=sem, vmem_limit_bytes=VMEM_LIMIT)


def _pick(total, candidates):
    for cand in candidates:
        if total % cand == 0:
            return cand
    raise ValueError(f"no tile for {total} in {candidates}")


def _dot(a, b):
    return jnp.dot(a, b, preferred_element_type=F32)


def _dot_nt(a, b):
    return lax.dot_general(a, b, (((1,), (1,)), ((), ())), preferred_element_type=F32)


def _sigmoid(x):
    return jax.nn.sigmoid(x)


def _ada_kernel(cc_ref, w_ref, b_ref, o_ref):
    a = cc_ref[...]
    a = a * _sigmoid(a)
    o_ref[0] = _dot(a.astype(BF16), w_ref[0].astype(BF16)) + b_ref[0]


def _ada(cc, w_ada, b_ada):
    depth, d, n = w_ada.shape
    tn = 1024
    return pl.pallas_call(
        _ada_kernel,
        grid=(depth, n // tn),
        in_specs=[pl.BlockSpec((8, d), lambda l, j: (0, 0)),
                  pl.BlockSpec((1, d, tn), lambda l, j: (l, 0, j)),
                  pl.BlockSpec((1, 1, tn), lambda l, j: (l, 0, j))],
        out_specs=pl.BlockSpec((1, 8, tn), lambda l, j: (l, 0, j)),
        out_shape=jax.ShapeDtypeStruct((depth, 8, n), F32),
        compiler_params=_params("parallel", "parallel"),
        name="ada",
    )(cc, w_ada, b_ada.reshape(depth, 1, n))


def _cast_kernel(w_ref, o_ref):
    o_ref[...] = w_ref[...].astype(o_ref.dtype)


def _w_in_gates_first(w_in):
    depth, d, n = w_in.shape
    tn = 256
    nb = n // tn
    shift = N_REST // tn
    assert n % tn == 0 and N_REST % tn == 0
    return pl.pallas_call(
        _cast_kernel,
        grid=(nb,),
        in_specs=[pl.BlockSpec((depth, d, tn), lambda j: (0, 0, (j + shift) % nb))],
        out_specs=pl.BlockSpec((depth, d, tn), lambda j: (0, 0, j)),
        out_shape=jax.ShapeDtypeStruct((depth, d, n), BF16),
        compiler_params=_params("parallel"),
        name="w_in_cast",
    )(w_in)


NORM_ROWS = 64


def _norm_modulate(x_ref, h_ref, h_row0, g_ref, mod_ref, shift_row, first_row, n_ctx, keep=None):
    nrows = x_ref.shape[0]
    chunk = min(NORM_ROWS, nrows)
    gain, shift = _mod_vectors(g_ref, mod_ref, shift_row)
    if keep is not None:
        gain = [v * keep for v in gain]
        shift = [v * keep for v in shift]

    def body(r, carry):
        r0 = pl.multiple_of(r * chunk, chunk)
        h_ref[pl.ds(h_row0 + r0, chunk), :] = _norm_rows(x_ref[pl.ds(r0, chunk), :], gain, shift, first_row + r0, n_ctx)
        return carry

    lax.fori_loop(0, nrows // chunk, body, 0)


def _mod_vectors(g_ref, mod_ref, shift_row):
    gain = [g_ref[...] * (1.0 + mod_ref[0, grp, shift_row + 1:shift_row + 2, :]) for grp in range(2)]
    shift = [mod_ref[0, grp, shift_row:shift_row + 1, :] for grp in range(2)]
    return gain, shift


def _norm_rows(x, gain, shift, first_row, n_ctx):
    y = x * lax.rsqrt(jnp.mean(x * x, axis=-1, keepdims=True) + EPS)
    is_ctx = first_row + lax.broadcasted_iota(jnp.int32, (x.shape[0], 1), 0) < n_ctx
    return (y * jnp.where(is_ctx, gain[1], gain[0]) + jnp.where(is_ctx, shift[1], shift[0])).astype(BF16)


def _proj_kernel(x0_ref, xn_ref, g_ref, mod_ref, w_ref, o_ref, h_ref, *, tm, n_ctx, shift_row, ahead):
    i = pl.program_id(0)
    j = pl.program_id(1)

    @pl.when(jnp.logical_and(i == 0, j == 0))
    def _():
        _norm_modulate(x0_ref, h_ref.at[0], 0, g_ref, mod_ref, shift_row, 0, n_ctx)

    gain, shift = _mod_vectors(g_ref, mod_ref, shift_row)
    for parity in range(2):
        @pl.when(lax.rem(i, 2) == parity)
        def _():
            o_ref[...] = _dot(h_ref[parity], w_ref[...]).astype(o_ref.dtype)
            r0 = pl.multiple_of(jnp.minimum(j, tm // ahead - 1) * ahead, ahead)
            for sub in range(ahead // NORM_ROWS):
                rr = r0 + sub * NORM_ROWS
                h_ref[1 - parity, pl.ds(rr, NORM_ROWS), :] = _norm_rows(
                    xn_ref[pl.ds(rr, NORM_ROWS), :], gain, shift, (i + 1) * tm + rr, n_ctx)


def _proj(x, g, mod, layer, shift_row, w, n_ctx, tn_candidates):
    t, d = x.shape
    n = w.shape[2]
    tm = _pick(t, (768, 256))
    tn = _pick(n, tn_candidates)
    nt = t // tm
    ahead = 2 * NORM_ROWS
    assert tm % ahead == 0 and n // tn >= tm // ahead
    return pl.pallas_call(
        functools.partial(_proj_kernel, tm=tm, n_ctx=n_ctx, shift_row=shift_row, ahead=ahead),
        grid=(nt, n // tn),
        in_specs=[pl.BlockSpec((tm, d), lambda i, j: (0, 0), pipeline_mode=pl.Buffered(1)),
                  pl.BlockSpec((tm, d), lambda i, j: (jnp.minimum(i + 1, nt - 1), 0)),
                  pl.BlockSpec((1, d), lambda i, j: (0, 0)),
                  pl.BlockSpec((1, 2, 8, d), lambda i, j: (layer, 0, 0, 0)),
                  pl.BlockSpec((None, d, tn), lambda i, j: (layer, 0, j))],
        out_specs=pl.BlockSpec((tm, tn), lambda i, j: (i, j)),
        out_shape=jax.ShapeDtypeStruct((t, n), BF16),
        scratch_shapes=[pltpu.VMEM((2, tm, d), BF16)],
        compiler_params=_params("arbitrary", "arbitrary"),
        name="proj",
    )(x, x, g.reshape(1, d), mod, w)


def _softplus(x):
    return jnp.maximum(x, 0.0) + jnp.log(1.0 + jnp.exp(-jnp.abs(x)))


N_SEG = 8


def _scan_rows(a_ref, b_ref, h_ref, carry_ref, reverse):
    nblk, _, width = a_ref.shape
    seg_len = h_ref.shape[0] // N_SEG
    pitch = _seg_pitch(seg_len)
    for c in range(nblk):
        cols = slice(c * width, (c + 1) * width)
        h = jnp.zeros((N_SEG, width), F32)
        p = jnp.ones((N_SEG, width), F32)
        for j in (reversed(range(seg_len)) if reverse else range(seg_len)):
            rows = pl.ds(j, N_SEG, stride=pitch)
            aj = a_ref[c, rows, :]
            h = aj * h + b_ref[c, rows, :]
            p = aj * p
            b_ref[c, rows, :] = h
            a_ref[c, rows, :] = p
        state = carry_ref[0:1, cols]
        for k in (reversed(range(N_SEG)) if reverse else range(N_SEG)):
            h_ref[k * seg_len:(k + 1) * seg_len, cols] = (
                b_ref[c, k * pitch:k * pitch + seg_len, :] + a_ref[c, k * pitch:k * pitch + seg_len, :] * state)
            state = h[k:k + 1, :] + p[k:k + 1, :] * state
        carry_ref[:, cols] = jnp.broadcast_to(state, (carry_ref.shape[0], width))


def _seg_pitch(seg_len):
    return seg_len + 8 if (seg_len // 8) % 2 == 0 else seg_len


def _rg_kernel(xfp, xfm, xfn, xbp, xbm, xbn, cw_ref, wbd_ref, brg_ref, lam_ref, hf_ref, hb_ref,
               ext, a_ref, b_ref, carry_f, carry_b, *, nchunks):
    s = pl.program_id(0)
    tm = xfm.shape[0]
    bw = xfm.shape[1]

    @pl.when(s == 0)
    def _():
        carry_f[...] = jnp.zeros_like(carry_f)
        carry_b[...] = jnp.zeros_like(carry_b)

    cb = jnp.where(s == 0, 0, nchunks - s)

    def gates(d, xp, xm, xn, cidx):
        prev_ok = (cidx >= 2).astype(F32)
        next_ok = jnp.logical_and(cidx >= 1, cidx <= nchunks - 2).astype(F32)
        ext[0:HALO, :] = xp[...].astype(F32) * prev_ok
        ext[HALO:HALO + tm, :] = xm[...].astype(F32)
        ext[HALO + tm:2 * HALO + tm, :] = xn[...].astype(F32) * next_ok
        left = RG_CONV // 2
        xl = cw_ref[0:1, :] * ext[HALO - left:HALO - left + tm, :]
        for k in range(1, RG_CONV):
            xl = xl + cw_ref[k:k + 1, :] * ext[HALO - left + k:HALO - left + k + tm, :]
        g = _sigmoid(_dot(xl.astype(BF16), wbd_ref[d]) + brg_ref[d])
        log_a = (-RG_C * g[:, :bw]) * _softplus(-lam_ref[d])
        a = jnp.exp(log_a)
        b = jnp.sqrt(-jnp.tanh(log_a) * (a * a + 1.0)) * g[:, bw:] * xl
        seg_len = tm // N_SEG
        pitch = _seg_pitch(seg_len)
        for c in range(bw // 128):
            for k in range(N_SEG):
                a_ref[d, c, k * pitch:k * pitch + seg_len, :] = a[k * seg_len:(k + 1) * seg_len, 128 * c:128 * (c + 1)]
                b_ref[d, c, k * pitch:k * pitch + seg_len, :] = b[k * seg_len:(k + 1) * seg_len, 128 * c:128 * (c + 1)]

    gates(0, xfp, xfm, xfn, s)
    gates(1, xbp, xbm, xbn, cb)
    _scan_rows(a_ref.at[0], b_ref.at[0], hf_ref, carry_f, reverse=False)
    _scan_rows(a_ref.at[1], b_ref.at[1], hb_ref, carry_b, reverse=True)


def _rglru(z, rg_conv, wbd, brg, lam):
    t = z.shape[0]
    tm = ROW_TILE
    nchunks = t // tm
    hpt = tm // HALO
    nh = t // HALO

    def bwd(s):
        return jnp.where(s == 0, 0, nchunks - s)

    def prev_map(cidx):
        return lambda s: (jnp.maximum(cidx(s) * hpt - 1, 0), COL_XR)

    def next_map(cidx):
        return lambda s: (jnp.minimum((cidx(s) + 1) * hpt, nh - 1), COL_XR)

    fwd = lambda s: s
    in_specs = [
        pl.BlockSpec((HALO, BRANCH_W), prev_map(fwd)),
        pl.BlockSpec((tm, BRANCH_W), lambda s: (s, COL_XR)),
        pl.BlockSpec((HALO, BRANCH_W), next_map(fwd)),
        pl.BlockSpec((HALO, BRANCH_W), prev_map(bwd)),
        pl.BlockSpec((tm, BRANCH_W), lambda s: (bwd(s), COL_XR)),
        pl.BlockSpec((HALO, BRANCH_W), next_map(bwd)),
        pl.BlockSpec((RG_CONV, BRANCH_W), lambda s: (0, 0)),
        pl.BlockSpec((2, BRANCH_W, 2 * BRANCH_W), lambda s: (0, 0, 0)),
        pl.BlockSpec((2, 1, 2 * BRANCH_W), lambda s: (0, 0, 0)),
        pl.BlockSpec((2, 1, BRANCH_W), lambda s: (0, 0, 0)),
    ]
    return pl.pallas_call(
        functools.partial(_rg_kernel, nchunks=nchunks),
        grid=(nchunks,),
        in_specs=in_specs,
        out_specs=[pl.BlockSpec((tm, BRANCH_W), lambda s: (s, 0)),
                   pl.BlockSpec((tm, BRANCH_W), lambda s: (bwd(s), 0))],
        out_shape=[jax.ShapeDtypeStruct((t, BRANCH_W), F32)] * 2,
        scratch_shapes=[pltpu.VMEM((tm + 2 * HALO, BRANCH_W), F32),
                        pltpu.VMEM((2, BRANCH_W // 128, N_SEG * _seg_pitch(tm // N_SEG), 128), F32),
                        pltpu.VMEM((2, BRANCH_W // 128, N_SEG * _seg_pitch(tm // N_SEG), 128), F32),
                        pltpu.VMEM((8, BRANCH_W), F32),
                        pltpu.VMEM((8, BRANCH_W), F32)],
        compiler_params=_params("arbitrary"),
        name="rglru",
    )(z, z, z, z, z, z, rg_conv, wbd, brg, lam)


def _conf_kernel(vp, vm, vn, gp, gm, gn, dw_ref, lng_ref, lnb_ref, o_ref, ext, shifted, *, nchunks):
    i = pl.program_id(0)
    tm = vm.shape[0]
    prev_ok = (i >= 2).astype(F32)
    next_ok = jnp.logical_and(i >= 1, i <= nchunks - 2).astype(F32)

    def glu(v, g):
        return v[...].astype(F32) * _sigmoid(g[...].astype(F32))

    ext[0:HALO, :] = glu(vp, gp) * prev_ok
    ext[HALO:HALO + tm, :] = glu(vm, gm)
    ext[HALO + tm:2 * HALO + tm, :] = glu(vn, gn) * next_ok
    base = HALO - CONF_WIDTH // 2
    span = tm + 2 * HALO - 8
    for r in range(1, 8):
        shifted[r - 1] = ext[r:r + span, :]
    acc = None
    for k in range(CONF_WIDTH):
        q, r = divmod(base + k, 8)
        rows = ext[8 * q:8 * q + tm, :] if r == 0 else shifted[r - 1, 8 * q:8 * q + tm, :]
        term = dw_ref[k:k + 1, :] * rows
        acc = term if acc is None else acc + term
    mu = jnp.mean(acc, axis=-1, keepdims=True)
    xc = acc - mu
    y = xc * lax.rsqrt(jnp.mean(xc * xc, axis=-1, keepdims=True) + EPS) * lng_ref[...] + lnb_ref[...]
    o_ref[...] = (y * _sigmoid(y)).astype(o_ref.dtype)


def _conformer(z, conf_dw, ln_g, ln_b):
    t = z.shape[0]
    tm = ROW_TILE
    nchunks = t // tm
    hpt = tm // HALO
    nh = t // HALO

    def specs(col):
        return [pl.BlockSpec((HALO, BRANCH_W), lambda i: (jnp.maximum(i * hpt - 1, 0), col)),
                pl.BlockSpec((tm, BRANCH_W), lambda i: (i, col)),
                pl.BlockSpec((HALO, BRANCH_W), lambda i: (jnp.minimum((i + 1) * hpt, nh - 1), col))]

    return pl.pallas_call(
        functools.partial(_conf_kernel, nchunks=nchunks),
        grid=(nchunks,),
        in_specs=specs(COL_CV) + specs(COL_CG) + [
            pl.BlockSpec((CONF_WIDTH, BRANCH_W), lambda i: (0, 0)),
            pl.BlockSpec((1, BRANCH_W), lambda i: (0, 0)),
            pl.BlockSpec((1, BRANCH_W), lambda i: (0, 0))],
        out_specs=pl.BlockSpec((tm, BRANCH_W), lambda i: (i, 0)),
        out_shape=jax.ShapeDtypeStruct((t, BRANCH_W), BF16),
        scratch_shapes=[pltpu.VMEM((tm + 2 * HALO, BRANCH_W), F32),
                        pltpu.VMEM((7, tm + 2 * HALO - 8, BRANCH_W), F32)],
        compiler_params=_params("parallel"),
        name="conformer",
    )(z, z, z, z, z, z, conf_dw, ln_g.reshape(1, -1), ln_b.reshape(1, -1))


def _na_bias_table(rpb):
    i = np.arange(NA_ROWS)[:, None]
    j = np.arange(NA_KEY_ROWS)[None, :]
    variants = [(np.zeros_like(i), i), (i, NA_ROWS + i), (np.full_like(i, NA_ROWS), 2 * NA_ROWS + i)]
    c = np.arange(GRID_W)[:, None]
    kc = np.arange(GRID_W)[None, :]
    col_start = np.clip(c - WIN_COLS // 2, 0, GRID_W - WIN_COLS)
    col_ok = (kc >= col_start) & (kc < col_start + WIN_COLS)
    nh = rpb.shape[0]
    padded = jnp.pad(rpb.astype(F32), ((0, 0), (0, 0), (GRID_W, GRID_W)))
    off = GRID_W + WIN_COLS - 1
    toeplitz = jnp.stack([padded[:, :, off - cc:off - cc + GRID_W] for cc in range(GRID_W)], axis=2)
    toeplitz = jnp.where(col_ok[None, None], toeplitz, NEG)
    masked = jnp.full((nh, GRID_W, GRID_W), NEG, F32)
    tables = [jnp.full((nh, ROW_TILE, NA_KEY_ROWS * GRID_W), NEG, F32)]
    for start, qrow in variants:
        row_ok = (j >= start) & (j < start + WIN_ROWS)
        dr = j - qrow + WIN_ROWS - 1
        rows = [jnp.concatenate([toeplitz[:, dr[qi, kj]] if row_ok[qi, kj] else masked for kj in range(NA_KEY_ROWS)],
                                axis=-1) for qi in range(NA_ROWS)]
        tables.append(jnp.stack(rows, axis=1).reshape(nh, ROW_TILE, NA_KEY_ROWS * GRID_W))
    return jnp.stack(tables)


def _na_kernel(q_ref, k0, k1, k2, kc, v0, v1, v2, vc, bias_ref, o_ref):
    tm = q_ref.shape[0]
    lane = lax.broadcasted_iota(jnp.int32, (1, 128), 1)
    scale = HEAD_DIM ** -0.5
    qmask = [jnp.where(lane < HEAD_DIM, scale, 0.0).astype(BF16), jnp.where(lane >= HEAD_DIM, scale, 0.0).astype(BF16)]
    first_half = lax.broadcasted_iota(jnp.int32, (tm, 128), 1) < HEAD_DIM
    one = jnp.ones((tm, 128), BF16)

    def scores(h):
        cols = slice(128 * (h // 2), 128 * (h // 2 + 1))
        qm = q_ref[:, cols] * qmask[h % 2]
        s = [_dot_nt(qm, r[:, cols]) for r in (k0, k1, k2, kc)]
        return [s[jb] + bias_ref[0, h, :, tm * jb:tm * (jb + 1)] for jb in range(3)] + [s[3]]

    n_heads = BRANCH_W // HEAD_DIM
    s_next = scores(0)
    pair = []
    for h in range(n_heads):
        s = s_next
        if h + 1 < n_heads:
            s_next = scores(h + 1)
        cols = slice(128 * (h // 2), 128 * (h // 2 + 1))
        m = jnp.max(jnp.maximum(jnp.maximum(s[0], s[1]), jnp.maximum(s[2], s[3])), axis=-1, keepdims=True)
        o = None
        for sj, r in zip(s, (v0, v1, v2, vc)):
            va = jnp.where(first_half, r[:, cols], one) if h % 2 == 0 else jnp.where(first_half, one, r[:, cols])
            term = _dot(jnp.exp(sj - m).astype(BF16), va)
            o = term if o is None else o + term
        pair.append(o)
        if h % 2 == 1:
            num = jnp.where(first_half, pair[0], pair[1])
            den = jnp.where(first_half, pltpu.roll(pair[0], HEAD_DIM, 1), pltpu.roll(pair[1], HEAD_DIM, 1))
            o_ref[:, cols] = (num * (1.0 / den)).astype(o_ref.dtype)
            pair = []


def _natten(z, bias):
    t = z.shape[0]
    tm = ROW_TILE
    nt = t // tm
    assert nt >= 4

    def kbase(i):
        return jnp.clip(i - 1, 1, nt - 3)

    def variant(i):
        return jnp.where(i == 0, 0, jnp.where(i == 1, 1, jnp.where(i == nt - 1, 3, 2)))

    def kv_specs(col):
        return [pl.BlockSpec((tm, BRANCH_W), lambda i: (kbase(i), col)),
                pl.BlockSpec((tm, BRANCH_W), lambda i: (kbase(i) + 1, col)),
                pl.BlockSpec((tm, BRANCH_W), lambda i: (kbase(i) + 2, col)),
                pl.BlockSpec((tm, BRANCH_W), lambda i: (0, col))]

    nh = bias.shape[1]
    return pl.pallas_call(
        _na_kernel,
        grid=(nt,),
        in_specs=[pl.BlockSpec((tm, BRANCH_W), lambda i: (i, COL_QA))] + kv_specs(COL_KA) + kv_specs(COL_VA) + [
            pl.BlockSpec((1, nh, tm, 3 * tm), lambda i: (variant(i), 0, 0, 0))],
        out_specs=pl.BlockSpec((tm, BRANCH_W), lambda i: (i, 0)),
        out_shape=jax.ShapeDtypeStruct((t, BRANCH_W), BF16),
        compiler_params=_params("parallel"),
        name="natten",
    )(z, z, z, z, z, z, z, z, z, bias)


def _rope_tables(seq, n_ctx):
    half = HEAD_DIM // 2
    n_f = half // 2
    inv_freq = ROPE_THETA ** (-jnp.arange(n_f, dtype=F32) / n_f)
    pos = jnp.arange(seq)
    ang_r = (pos // GRID_W).astype(F32)[:, None] * inv_freq[None, :]
    ang_c = (pos % GRID_W).astype(F32)[:, None] * inv_freq[None, :]
    cos = jnp.concatenate([jnp.cos(ang_r), jnp.cos(ang_r), jnp.cos(ang_c), jnp.cos(ang_c)], axis=-1)
    sin = jnp.concatenate([-jnp.sin(ang_r), jnp.sin(ang_r), -jnp.sin(ang_c), jnp.sin(ang_c)], axis=-1)
    cos = jnp.concatenate([jnp.ones((n_ctx, HEAD_DIM), F32), cos], axis=0)
    sin = jnp.concatenate([jnp.zeros((n_ctx, HEAD_DIM), F32), sin], axis=0)
    return jnp.tile(cos, (1, 2)), jnp.tile(sin, (1, 2))


def _qkprep_kernel(q_ref, k_ref, v_ref, cos_ref, sin_ref, qg_ref, kg_ref, aq_ref, ak_ref, qo_ref, ko_ref, vo_ref):
    n_f = HEAD_DIM // 4

    def norm_rope(x, g, avg, cos, sin):
        n = x.shape[1]
        x2 = x * x
        hi = x2.astype(BF16)
        lo = (x2 - hi.astype(F32)).astype(BF16)
        ms = _dot(hi, avg) + _dot(lo, avg)
        y = x * lax.rsqrt(ms + EPS) * g
        lane = lax.broadcasted_iota(jnp.int32, x.shape, 1) % (2 * n_f)
        partner = jnp.where(lane < n_f, pltpu.roll(y, n - n_f, 1), pltpu.roll(y, n_f, 1))
        return y * cos + partner * sin

    cos = cos_ref[...]
    sin = sin_ref[...]
    reps = q_ref.shape[1] // 128
    q = norm_rope(q_ref[...].astype(F32), qg_ref[...], aq_ref[...],
                  jnp.concatenate([cos] * reps, axis=1), jnp.concatenate([sin] * reps, axis=1))
    qo_ref[...] = (q * HEAD_DIM ** -0.5).astype(BF16)

    first = lax.broadcasted_iota(jnp.int32, k_ref.shape, 1) < HEAD_DIM
    k = norm_rope(k_ref[...].astype(F32), kg_ref[...], ak_ref[...], cos, sin)
    k_sw = pltpu.roll(k, HEAD_DIM, 1)
    kdup = jnp.concatenate([jnp.where(first, k, k_sw), jnp.where(first, k_sw, k)], axis=1)
    ko_ref[...] = kdup.T.astype(BF16)
    v = v_ref[...].astype(F32)
    v_sw = pltpu.roll(v, HEAD_DIM, 1)
    vo_ref[...] = jnp.concatenate([jnp.where(first, v, 1.0), jnp.where(first, 1.0, v_sw),
                                   jnp.where(first, v_sw, 1.0), jnp.where(first, 1.0, v)], axis=1).astype(BF16)


def _head_avg(n):
    idx = np.arange(n) // HEAD_DIM
    return jnp.asarray((idx[:, None] == idx[None, :]).astype(np.float32) / HEAD_DIM, BF16)


def _qkprep(z, cos, sin, q_norm_g, k_norm_g):
    t = z.shape[0]
    tm = ROW_TILE
    kvw = GQA_KV_HEADS * HEAD_DIM
    qg = jnp.tile(q_norm_g.astype(F32), BRANCH_W // HEAD_DIM).reshape(1, BRANCH_W)
    kg = jnp.tile(k_norm_g.astype(F32), GQA_KV_HEADS).reshape(1, kvw)
    const = lambda i: (0, 0)
    return pl.pallas_call(
        _qkprep_kernel,
        grid=(t // tm,),
        in_specs=[pl.BlockSpec((tm, BRANCH_W), lambda i: (i, COL_QD)),
                  pl.BlockSpec((tm, kvw), lambda i: (i, COL_KD)),
                  pl.BlockSpec((tm, kvw), lambda i: (i, COL_VD)),
                  pl.BlockSpec((tm, kvw), lambda i: (i, 0)),
                  pl.BlockSpec((tm, kvw), lambda i: (i, 0)),
                  pl.BlockSpec((1, BRANCH_W), const),
                  pl.BlockSpec((1, kvw), const),
                  pl.BlockSpec((BRANCH_W, BRANCH_W), const),
                  pl.BlockSpec((kvw, kvw), const)],
        out_specs=[pl.BlockSpec((tm, BRANCH_W), lambda i: (i, 0)),
                   pl.BlockSpec((2 * kvw, tm), lambda i: (0, i)),
                   pl.BlockSpec((tm, 4 * kvw), lambda i: (i, 0))],
        out_shape=[jax.ShapeDtypeStruct((t, BRANCH_W), BF16),
                   jax.ShapeDtypeStruct((2 * kvw, t), BF16),
                   jax.ShapeDtypeStruct((t, 4 * kvw), BF16)],
        compiler_params=_params("parallel"),
        name="qkprep",
    )(z, z, z, cos, sin, qg, kg, _head_avg(BRANCH_W), _head_avg(kvw))


def _gqa_kernel(q_ref, kt_ref, va_ref, o_ref, qs_ref, s_ref, m_ref, acc_ref, *, n_ctx, n_chunks, tk):
    tq = q_ref.shape[0]
    lane = lax.broadcasted_iota(jnp.int32, (1, 128), 1)
    hmask = [(lane < HEAD_DIM).astype(BF16), (lane >= HEAD_DIM).astype(BF16)]
    first_half = lax.broadcasted_iota(jnp.int32, (tq, 128), 1) < HEAD_DIM
    group = BRANCH_W // HEAD_DIM // GQA_KV_HEADS
    for g in range(GQA_KV_HEADS):
        for hh in range(group):
            pair = q_ref[:, 128 * (g * group // 2 + hh // 2):128 * (g * group // 2 + hh // 2 + 1)]
            qs_ref[hh] = pair * hmask[hh % 2]

        def scores(off, size, slot):
            for hh in range(group):
                s_ref[slot, hh, :, 0:size] = _dot(qs_ref[hh], kt_ref[128 * g:128 * (g + 1), pl.ds(off, size)])

        def update(off, size, slot, first):
            for hh in range(group):
                s = s_ref[slot, hh, :, 0:size]
                mx = s[:, 0:128]
                for cc in range(1, size // 128):
                    mx = jnp.maximum(mx, s[:, 128 * cc:128 * (cc + 1)])
                rmax = jnp.max(mx, axis=-1, keepdims=True)
                vcol = 256 * g + 128 * (hh % 2)
                va = va_ref[pl.ds(off, size), vcol:vcol + 128]
                if first:
                    m_new = jnp.broadcast_to(rmax, (tq, 128))
                    p = jnp.exp(s - jnp.concatenate([m_new] * (size // 128), axis=1))
                    acc_ref[hh] = _dot(p.astype(BF16), va)
                else:
                    m_old = m_ref[hh]
                    m_new = jnp.maximum(m_old, rmax)
                    p = jnp.exp(s - jnp.concatenate([m_new] * (size // 128), axis=1))
                    acc_ref[hh] = jnp.exp(m_old - m_new) * acc_ref[hh] + _dot(p.astype(BF16), va)
                m_ref[hh] = m_new

        scores(0, n_ctx, 1)
        if n_chunks:
            scores(n_ctx, tk, 0)
        update(0, n_ctx, 1, True)

        def body(c2, carry):
            off = pl.multiple_of(n_ctx + 2 * c2 * tk, 256)
            scores(off + tk, tk, 1)
            update(off, tk, 0, False)
            scores(off + 2 * tk, tk, 0)
            update(off + tk, tk, 1, False)
            return carry

        if n_chunks:
            lax.fori_loop(0, n_chunks // 2 - 1, body, 0)
            off = n_ctx + (n_chunks - 2) * tk
            scores(off + tk, tk, 1)
            update(off, tk, 0, False)
            update(off + tk, tk, 1, False)

        for pr in range(group // 2):
            o0 = acc_ref[2 * pr]
            o1 = acc_ref[2 * pr + 1]
            num = jnp.where(first_half, o0, o1)
            den = jnp.where(first_half, pltpu.roll(o0, HEAD_DIM, 1), pltpu.roll(o1, HEAD_DIM, 1))
            col = 128 * (g * group // 2 + pr)
            o_ref[:, col:col + 128] = (num * (1.0 / den)).astype(o_ref.dtype)


def _gqa(qn, kt, va, n_ctx):
    t = qn.shape[0]
    seq = t - n_ctx
    tq = _pick(seq, (512, 256))
    tk = _pick(seq, (2048, 1024, 512)) // 2
    assert tk >= n_ctx
    group = BRANCH_W // HEAD_DIM // GQA_KV_HEADS
    resident = [pl.BlockSpec(kt.shape, lambda i: (0, 0), pipeline_mode=pl.Buffered(1)),
                pl.BlockSpec(va.shape, lambda i: (0, 0), pipeline_mode=pl.Buffered(1))]

    def scratch(rows, width):
        return [pltpu.VMEM((group, rows, 128), BF16),
                pltpu.VMEM((2, group, rows, width), F32),
                pltpu.VMEM((group, rows, 128), F32),
                pltpu.VMEM((group, rows, 128), F32)]

    out_ctx = pl.pallas_call(
        functools.partial(_gqa_kernel, n_ctx=n_ctx, n_chunks=0, tk=tk),
        grid=(1,),
        in_specs=[pl.BlockSpec((n_ctx, BRANCH_W), lambda i: (0, 0))] + resident,
        out_specs=pl.BlockSpec((n_ctx, BRANCH_W), lambda i: (0, 0)),
        out_shape=jax.ShapeDtypeStruct((n_ctx, BRANCH_W), BF16),
        scratch_shapes=scratch(n_ctx, n_ctx),
        compiler_params=_params("arbitrary"),
        name="gqa_ctx",
    )(qn, kt, va)
    out_lat = pl.pallas_call(
        functools.partial(_gqa_kernel, n_ctx=n_ctx, n_chunks=seq // tk, tk=tk),
        grid=(seq // tq,),
        in_specs=[pl.BlockSpec((pl.Element(tq), pl.Element(BRANCH_W)),
                               lambda i: (pl.multiple_of(n_ctx + i * tq, ROW_TILE), 0))] + resident,
        out_specs=pl.BlockSpec((tq, BRANCH_W), lambda i: (i, 0)),
        out_shape=jax.ShapeDtypeStruct((seq, BRANCH_W), BF16),
        scratch_shapes=scratch(tq, tk),
        compiler_params=_params("parallel"),
        name="gqa",
    )(qn, kt, va)
    return out_ctx, out_lat


def _gelu_tanh(x):
    return 0.5 * x * (1.0 + jnp.tanh(np.sqrt(2.0 / np.pi).astype(np.float32) * (x + 0.044715 * (x * x * x))))


def _merge_kernel(a_ref, hf_ref, hb_ref, gr_ref, c_ref, dc_ref, dl_ref, g0, g1, g2, g3, x_ref, mod_ref, wb_ref, wo_ref,
                  o_ref, *, nctx_tiles):
    b = ((hf_ref[...] + hb_ref[...]) * _gelu_tanh(gr_ref[...].astype(F32))).astype(BF16)
    d = jnp.where(pl.program_id(0) < nctx_tiles, dc_ref[...], dl_ref[...])
    branches = (a_ref[...], b, c_ref[...], d)
    gates = (g0, g1, g2, g3)
    y = None
    for n in range(N_BRANCH):
        term = _sigmoid(gates[n][...].astype(F32)) * _dot(branches[n], wb_ref[n])
        y = term if y is None else y + term
    out = _dot(y.astype(BF16), wo_ref[...])
    o_ref[...] = x_ref[...] + mod_ref[0, 0, 2:3, :] * out


def _merge(a, hf, hb, z, c, d_ctx, d_lat, x, mod, layer, wb, wo, n_ctx):
    t, d = x.shape
    tm = ROW_TILE
    nctx_tiles = n_ctx // tm
    assert nctx_tiles == 1
    row = lambda i: (i, 0)
    single = pl.Buffered(1)
    return pl.pallas_call(
        functools.partial(_merge_kernel, nctx_tiles=nctx_tiles),
        grid=(t // tm,),
        in_specs=[pl.BlockSpec((tm, BRANCH_W), row),
                  pl.BlockSpec((tm, BRANCH_W), row),
                  pl.BlockSpec((tm, BRANCH_W), row),
                  pl.BlockSpec((tm, BRANCH_W), lambda i: (i, COL_GR)),
                  pl.BlockSpec((tm, BRANCH_W), row),
                  pl.BlockSpec((tm, BRANCH_W), lambda i: (0, 0)),
                  pl.BlockSpec((tm, BRANCH_W), lambda i: (jnp.maximum(i - nctx_tiles, 0), 0))] + [
                      pl.BlockSpec((tm, d), functools.partial(lambda n, i: (i, n), n)) for n in range(N_BRANCH)] + [
                  pl.BlockSpec((tm, d), row),
                  pl.BlockSpec((1, 1, 8, d), lambda i: (layer, (i < nctx_tiles).astype(jnp.int32), 0, 0)),
                  pl.BlockSpec((None, N_BRANCH, BRANCH_W, d), lambda i: (layer, 0, 0, 0), pipeline_mode=single),
                  pl.BlockSpec((None, d, d), lambda i: (layer, 0, 0), pipeline_mode=single)],
        out_specs=pl.BlockSpec((tm, d), row),
        out_shape=jax.ShapeDtypeStruct((t, d), F32),
        compiler_params=_params("parallel"),
        name="merge",
    )(a, hf, hb, z, c, d_ctx, d_lat, z, z, z, z, x, mod, wb, wo)


def _ffn_up_kernel(xp_ref, x_ref, xn_ref, g_ref, mod_ref, wg_ref, wv_ref, dwg_ref, dwv_ref, o_ref,
                   h_ref, ug_ref, uv_ref, *, tm, n_ctx):
    i = pl.program_id(0)

    @pl.when(pl.program_id(1) == 0)
    def _():
        prev_ok = (i > 0).astype(F32)
        next_ok = (i < pl.num_programs(0) - 1).astype(F32)
        _norm_modulate(xp_ref, h_ref, 0, g_ref, mod_ref, 3, i * tm - HALO, n_ctx, keep=prev_ok)
        _norm_modulate(x_ref, h_ref, HALO, g_ref, mod_ref, 3, i * tm, n_ctx)
        _norm_modulate(xn_ref, h_ref, HALO + tm, g_ref, mod_ref, 3, (i + 1) * tm, n_ctx, keep=next_ok)

    rows = i * tm + lax.broadcasted_iota(jnp.int32, (tm, 1), 0)
    keep_prev = (rows != n_ctx).astype(F32)
    keep_next = (rows != n_ctx - 1).astype(F32)

    def conv(u_ref, w_ref, dw_ref):
        u_ref[...] = _dot(h_ref[...], w_ref[...])
        prev = u_ref[HALO - 1:HALO - 1 + tm, :] * keep_prev
        nxt = u_ref[HALO + 1:HALO + 1 + tm, :] * keep_next
        return dw_ref[0:1, :] * prev + dw_ref[1:2, :] * u_ref[HALO:HALO + tm, :] + dw_ref[2:3, :] * nxt

    ug = conv(ug_ref, wg_ref, dwg_ref)
    uv = conv(uv_ref, wv_ref, dwv_ref)
    o_ref[...] = (ug * _sigmoid(ug) * uv).astype(o_ref.dtype)


def _ffn_up(x, g, mod, layer, w_up, ffn_dw, n_ctx):
    t, d = x.shape
    dff = w_up.shape[2] // 2
    tm = _pick(t, (768, 256))
    tn = _pick(dff, (512, 256))
    nj = dff // tn
    hpt = tm // HALO
    nh = t // HALO
    return pl.pallas_call(
        functools.partial(_ffn_up_kernel, tm=tm, n_ctx=n_ctx),
        grid=(t // tm, nj),
        in_specs=[pl.BlockSpec((HALO, d), lambda i, j: (jnp.maximum(i * hpt - 1, 0), 0)),
                  pl.BlockSpec((tm, d), lambda i, j: (i, 0)),
                  pl.BlockSpec((HALO, d), lambda i, j: (jnp.minimum((i + 1) * hpt, nh - 1), 0)),
                  pl.BlockSpec((1, d), lambda i, j: (0, 0)),
                  pl.BlockSpec((1, 2, 8, d), lambda i, j: (layer, 0, 0, 0)),
                  pl.BlockSpec((None, d, tn), lambda i, j: (layer, 0, j)),
                  pl.BlockSpec((None, d, tn), lambda i, j: (layer, 0, j + nj)),
                  pl.BlockSpec((3, tn), lambda i, j: (0, j)),
                  pl.BlockSpec((3, tn), lambda i, j: (0, j + nj))],
        out_specs=pl.BlockSpec((tm, tn), lambda i, j: (i, j)),
        out_shape=jax.ShapeDtypeStruct((t, dff), BF16),
        scratch_shapes=[pltpu.VMEM((tm + 2 * HALO, d), BF16),
                        pltpu.VMEM((tm + 2 * HALO, tn), F32),
                        pltpu.VMEM((tm + 2 * HALO, tn), F32)],
        compiler_params=_params("parallel", "arbitrary"),
        name="ffn_up",
    )(x, x, x, g.reshape(1, d), mod, w_up, w_up, ffn_dw, ffn_dw)


def _ffn_down_kernel(a_ref, w_ref, x_ref, mod_ref, o_ref, *, tm, n_ctx):
    rows = pl.program_id(0) * tm + lax.broadcasted_iota(jnp.int32, (tm, 1), 0)
    gate = jnp.where(rows < n_ctx, mod_ref[0, 1, 5:6, :], mod_ref[0, 0, 5:6, :])
    o_ref[...] = x_ref[...] + gate * _dot(a_ref[...], w_ref[...])


def _ffn_down(act, wd, x, mod, layer, n_ctx):
    t, d = x.shape
    dff = wd.shape[1]
    tm = _pick(t, (768, 256))
    tn = 512
    return pl.pallas_call(
        functools.partial(_ffn_down_kernel, tm=tm, n_ctx=n_ctx),
        grid=(t // tm, d // tn),
        in_specs=[pl.BlockSpec((tm, dff), lambda i, j: (i, 0)),
                  pl.BlockSpec((None, dff, tn), lambda i, j: (layer, 0, j)),
                  pl.BlockSpec((tm, tn), lambda i, j: (i, j)),
                  pl.BlockSpec((1, 2, 8, tn), lambda i, j: (layer, 0, 0, j))],
        out_specs=pl.BlockSpec((tm, tn), lambda i, j: (i, j)),
        out_shape=jax.ShapeDtypeStruct((t, d), F32),
        compiler_params=_params("parallel", "arbitrary"),
        name="ffn_down",
    )(act, wd, x, mod)


def _final_kernel(x_ref, g_ref, o_ref):
    x = x_ref[...]
    o_ref[...] = x * lax.rsqrt(jnp.mean(x * x, axis=-1, keepdims=True) + EPS) * g_ref[...]


def _final_norm(x, g, n_ctx):
    t, d = x.shape
    tm = ROW_TILE
    skip = n_ctx // tm
    return pl.pallas_call(
        _final_kernel,
        grid=((t - n_ctx) // tm,),
        in_specs=[pl.BlockSpec((tm, d), lambda i: (i + skip, 0)),
                  pl.BlockSpec((1, d), lambda i: (0, 0))],
        out_specs=pl.BlockSpec((tm, d), lambda i: (i, 0)),
        out_shape=jax.ShapeDtypeStruct((t - n_ctx, d), F32),
        compiler_params=_params("parallel"),
        name="final_norm",
    )(x, g.reshape(1, d))


def _block_diag(w):
    nb, bi, bj = w.shape
    eye = jnp.eye(nb, dtype=w.dtype)
    return jnp.einsum('nij,nm->nimj', w, eye).reshape(nb * bi, nb * bj)


def kernel(x, c, ctx, c_ctx, w_ada, b_ada, g_mix, w_in, na_rpb, rg_conv, w_rg, b_rg, rg_lambda, conf_dw, conf_ln_g,
           conf_ln_b, q_norm_g, k_norm_g, w_branch, w_out, g_ffn, w_up, ffn_dw, w_down, g_final):
    bsz, seq, d = x.shape
    n_ctx = ctx.shape[1]
    assert bsz == 1 and d == D_MODEL and n_ctx == ROW_TILE and seq % ROW_TILE == 0
    assert seq // GRID_W >= NA_KEY_ROWS

    cc = jnp.concatenate([c, c_ctx[None, :], jnp.zeros((6, d), F32)], axis=0)
    mod = _ada(cc, w_ada, b_ada)[:, :2, :].reshape(DEPTH, 2, N_MOD, d)
    mod = jnp.pad(mod, ((0, 0), (0, 0), (0, 8 - N_MOD), (0, 0)))

    xs = jnp.concatenate([ctx[0], x[0]], axis=0)
    cos, sin = _rope_tables(seq, n_ctx)

    w_in_b = _w_in_gates_first(w_in)
    w_branch_b, w_out_b, w_up_b, w_down_b = (w.astype(BF16) for w in (w_branch, w_out, w_up, w_down))

    for l in range(DEPTH):
        z = _proj(xs, g_mix[l], mod, l, 0, w_in_b, n_ctx, (1792, 896, 256))

        wbd = jnp.stack([jnp.concatenate([_block_diag(w_rg[l, dd, 0]), _block_diag(w_rg[l, dd, 1])], axis=1)
                         for dd in range(2)]).astype(BF16)
        brg = b_rg[l].reshape(2, 1, 2 * BRANCH_W)
        hf, hb = _rglru(z, rg_conv[l], wbd, brg, rg_lambda[l].reshape(2, 1, BRANCH_W))
        c_br = _conformer(z, conf_dw[l], conf_ln_g[l], conf_ln_b[l])
        a_br = _natten(z, _na_bias_table(na_rpb[l]))
        qn, kt, va = _qkprep(z, cos, sin, q_norm_g[l], k_norm_g[l])
        d_ctx, d_lat = _gqa(qn, kt, va, n_ctx)
        xs = _merge(a_br, hf, hb, z, c_br, d_ctx, d_lat, xs, mod, l, w_branch_b, w_out_b, n_ctx)

        act = _ffn_up(xs, g_ffn[l], mod, l, w_up_b, ffn_dw[l], n_ctx)
        xs = _ffn_down(act, w_down_b, xs, mod, l, n_ctx)

    return _final_norm(xs, g_final, n_ctx)[None]
```

```python
import functools

import numpy as np
import jax
import jax.numpy as jnp
from jax import lax
from jax.experimental import pallas as pl
from jax.experimental.pallas import tpu as pltpu

F32 = jnp.float32
BF16 = jnp.bfloat16

D_MODEL = 2048
DEPTH = 2
GRID_W = 64
HEAD_DIM = 64
N_BRANCH = 4
BRANCH_W = D_MODEL // N_BRANCH
WIN_ROWS = 8
WIN_COLS = 16
RG_C = 8.0
RG_CONV = 4
CONF_WIDTH = 31
GQA_KV_HEADS = 2
ROPE_THETA = 10000.0
D_FF = 5632
EPS = 1e-6
N_MOD = 6
N_GATE = N_BRANCH * D_MODEL
N_REST = 3 * BRANCH_W + 2 * BRANCH_W + 2 * BRANCH_W + BRANCH_W + 2 * GQA_KV_HEADS * HEAD_DIM
N_IN = N_GATE + N_REST

ROW_TILE = 256
HALO = 16
NA_ROWS = ROW_TILE // GRID_W
NA_KEY_ROWS = 3 * NA_ROWS
NEG = -1e30
VMEM_LIMIT = 56 * 1024 * 1024

COL_QA, COL_KA, COL_VA, COL_XR, COL_GR, COL_CV, COL_CG, COL_QD = (N_GATE // BRANCH_W + n for n in range(8))
COL_KD = (N_GATE + 8 * BRANCH_W) // 128
COL_VD = COL_KD + 1


def _params(*sem):
    return pltpu.CompilerParams(dimension_semantics=sem, vmem_limit_bytes=VMEM_LIMIT)


def _pick(total, candidates):
    for cand in candidates:
        if total % cand == 0:
            return cand
    raise ValueError(f"no tile for {total} in {candidates}")


def _dot(a, b):
    return jnp.dot(a, b, preferred_element_type=F32)


def _dot_nt(a, b):
    return lax.dot_general(a, b, (((1,), (1,)), ((), ())), preferred_element_type=F32)


def _sigmoid(x):
    return jax.nn.sigmoid(x)


def _ada_kernel(cc_ref, w_ref, b_ref, o_ref):
    a = cc_ref[...]
    a = a * _sigmoid(a)
    o_ref[0] = _dot(a.astype(BF16), w_ref[0].astype(BF16)) + b_ref[0]


def _ada(cc, w_ada, b_ada):
    depth, d, n = w_ada.shape
    tn = 1024
    return pl.pallas_call(
        _ada_kernel,
        grid=(depth, n // tn),
        in_specs=[pl.BlockSpec((8, d), lambda l, j: (0, 0)),
                  pl.BlockSpec((1, d, tn), lambda l, j: (l, 0, j)),
                  pl.BlockSpec((1, 1, tn), lambda l, j: (l, 0, j))],
        out_specs=pl.BlockSpec((1, 8, tn), lambda l, j: (l, 0, j)),
        out_shape=jax.ShapeDtypeStruct((depth, 8, n), F32),
        compiler_params=_params("parallel", "parallel"),
        name="ada",
    )(cc, w_ada, b_ada.reshape(depth, 1, n))


def _cast_kernel(w_ref, o_ref):
    o_ref[...] = w_ref[...].astype(o_ref.dtype)


def _w_in_gates_first(w_in):
    depth, d, n = w_in.shape
    tn = 256
    nb = n // tn
    shift = N_REST // tn
    assert n % tn == 0 and N_REST % tn == 0
    return pl.pallas_call(
        _cast_kernel,
        grid=(nb,),
        in_specs=[pl.BlockSpec((depth, d, tn), lambda j: (0, 0, (j + shift) % nb))],
        out_specs=pl.BlockSpec((depth, d, tn), lambda j: (0, 0, j)),
        out_shape=jax.ShapeDtypeStruct((depth, d, n), BF16),
        compiler_params=_params("parallel"),
        name="w_in_cast",
    )(w_in)


NORM_ROWS = 64


def _norm_modulate(x_ref, h_ref, h_row0, g_ref, mod_ref, shift_row, first_row, n_ctx, keep=None):
    nrows = x_ref.shape[0]
    chunk = min(NORM_ROWS, nrows)
    gain, shift = _mod_vectors(g_ref, mod_ref, shift_row)
    if keep is not None:
        gain = [v * keep for v in gain]
        shift = [v * keep for v in shift]

    def body(r, carry):
        r0 = pl.multiple_of(r * chunk, chunk)
        h_ref[pl.ds(h_row0 + r0, chunk), :] = _norm_rows(x_ref[pl.ds(r0, chunk), :], gain, shift, first_row + r0, n_ctx)
        return carry

    lax.fori_loop(0, nrows // chunk, body, 0)


def _mod_vectors(g_ref, mod_ref, shift_row):
    gain = [g_ref[...] * (1.0 + mod_ref[0, grp, shift_row + 1:shift_row + 2, :]) for grp in range(2)]
    shift = [mod_ref[0, grp, shift_row:shift_row + 1, :] for grp in range(2)]
    return gain, shift


def _norm_rows(x, gain, shift, first_row, n_ctx):
    y = x * lax.rsqrt(jnp.mean(x * x, axis=-1, keepdims=True) + EPS)
    is_ctx = first_row + lax.broadcasted_iota(jnp.int32, (x.shape[0], 1), 0) < n_ctx
    return (y * jnp.where(is_ctx, gain[1], gain[0]) + jnp.where(is_ctx, shift[1], shift[0])).astype(BF16)


def _proj_kernel(x0_ref, xn_ref, g_ref, mod_ref, w_ref, o_ref, h_ref, *, tm, n_ctx, shift_row, ahead):
    i = pl.program_id(0)
    j = pl.program_id(1)

    @pl.when(jnp.logical_and(i == 0, j == 0))
    def _():
        _norm_modulate(x0_ref, h_ref.at[0], 0, g_ref, mod_ref, shift_row, 0, n_ctx)

    gain, shift = _mod_vectors(g_ref, mod_ref, shift_row)
    for parity in range(2):
        @pl.when(lax.rem(i, 2) == parity)
        def _():
            o_ref[...] = _dot(h_ref[parity], w_ref[...]).astype(o_ref.dtype)
            r0 = pl.multiple_of(jnp.minimum(j, tm // ahead - 1) * ahead, ahead)
            for sub in range(ahead // NORM_ROWS):
                rr = r0 + sub * NORM_ROWS
                h_ref[1 - parity, pl.ds(rr, NORM_ROWS), :] = _norm_rows(
                    xn_ref[pl.ds(rr, NORM_ROWS), :], gain, shift, (i + 1) * tm + rr, n_ctx)


def _proj(x, g, mod, layer, shift_row, w, n_ctx, tn_candidates):
    t, d = x.shape
    n = w.shape[2]
    tm = _pick(t, (768, 256))
    tn = _pick(n, tn_candidates)
    nt = t // tm
    ahead = 2 * NORM_ROWS
    assert tm % ahead == 0 and n // tn >= tm // ahead
    return pl.pallas_call(
        functools.partial(_proj_kernel, tm=tm, n_ctx=n_ctx, shift_row=shift_row, ahead=ahead),
        grid=(nt, n // tn),
        in_specs=[pl.BlockSpec((tm, d), lambda i, j: (0, 0), pipeline_mode=pl.Buffered(1)),
                  pl.BlockSpec((tm, d), lambda i, j: (jnp.minimum(i + 1, nt - 1), 0)),
                  pl.BlockSpec((1, d), lambda i, j: (0, 0)),
                  pl.BlockSpec((1, 2, 8, d), lambda i, j: (layer, 0, 0, 0)),
                  pl.BlockSpec((None, d, tn), lambda i, j: (layer, 0, j))],
        out_specs=pl.BlockSpec((tm, tn), lambda i, j: (i, j)),
        out_shape=jax.ShapeDtypeStruct((t, n), BF16),
        scratch_shapes=[pltpu.VMEM((2, tm, d), BF16)],
        compiler_params=_params("arbitrary", "arbitrary"),
        name="proj",
    )(x, x, g.reshape(1, d), mod, w)


def _softplus(x):
    return jnp.maximum(x, 0.0) + jnp.log(1.0 + jnp.exp(-jnp.abs(x)))


N_SEG = 8


def _scan_rows(a_ref, b_ref, h_ref, carry_ref, reverse):
    nblk, _, width = a_ref.shape
    seg_len = h_ref.shape[0] // N_SEG
    pitch = _seg_pitch(seg_len)
    for c in range(nblk):
        cols = slice(c * width, (c + 1) * width)
        h = jnp.zeros((N_SEG, width), F32)
        p = jnp.ones((N_SEG, width), F32)
        for j in (reversed(range(seg_len)) if reverse else range(seg_len)):
            rows = pl.ds(j, N_SEG, stride=pitch)
            aj = a_ref[c, rows, :]
            h = aj * h + b_ref[c, rows, :]
            p = aj * p
            b_ref[c, rows, :] = h
            a_ref[c, rows, :] = p
        state = carry_ref[0:1, cols]
        for k in (reversed(range(N_SEG)) if reverse else range(N_SEG)):
            h_ref[k * seg_len:(k + 1) * seg_len, cols] = (
                b_ref[c, k * pitch:k * pitch + seg_len, :] + a_ref[c, k * pitch:k * pitch + seg_len, :] * state)
            state = h[k:k + 1, :] + p[k:k + 1, :] * state
        carry_ref[:, cols] = jnp.broadcast_to(state, (carry_ref.shape[0], width))


def _seg_pitch(seg_len):
    return seg_len + 8 if (seg_len // 8) % 2 == 0 else seg_len


def _rg_kernel(xfp, xfm, xfn, xbp, xbm, xbn, cw_ref, wbd_ref, brg_ref, lam_ref, hf_ref, hb_ref,
               ext, a_ref, b_ref, carry_f, carry_b, *, nchunks):
    s = pl.program_id(0)
    tm = xfm.shape[0]
    bw = xfm.shape[1]

    @pl.when(s == 0)
    def _():
        carry_f[...] = jnp.zeros_like(carry_f)
        carry_b[...] = jnp.zeros_like(carry_b)

    cb = jnp.where(s == 0, 0, nchunks - s)

    def gates(d, xp, xm, xn, cidx):
        prev_ok = (cidx >= 2).astype(F32)
        next_ok = jnp.logical_and(cidx >= 1, cidx <= nchunks - 2).astype(F32)
        ext[0:HALO, :] = xp[...].astype(F32) * prev_ok
        ext[HALO:HALO + tm, :] = xm[...].astype(F32)
        ext[HALO + tm:2 * HALO + tm, :] = xn[...].astype(F32) * next_ok
        left = RG_CONV // 2
        xl = cw_ref[0:1, :] * ext[HALO - left:HALO - left + tm, :]
        for k in range(1, RG_CONV):
            xl = xl + cw_ref[k:k + 1, :] * ext[HALO - left + k:HALO - left + k + tm, :]
        g = _sigmoid(_dot(xl.astype(BF16), wbd_ref[d]) + brg_ref[d])
        log_a = (-RG_C * g[:, :bw]) * _softplus(-lam_ref[d])
        a = jnp.exp(log_a)
        b = jnp.sqrt(-jnp.tanh(log_a) * (a * a + 1.0)) * g[:, bw:] * xl
        seg_len = tm // N_SEG
        pitch = _seg_pitch(seg_len)
        for c in range(bw // 128):
            for k in range(N_SEG):
                a_ref[d, c, k * pitch:k * pitch + seg_len, :] = a[k * seg_len:(k + 1) * seg_len, 128 * c:128 * (c + 1)]
                b_ref[d, c, k * pitch:k * pitch + seg_len, :] = b[k * seg_len:(k + 1) * seg_len, 128 * c:128 * (c + 1)]

    gates(0, xfp, xfm, xfn, s)
    gates(1, xbp, xbm, xbn, cb)
    _scan_rows(a_ref.at[0], b_ref.at[0], hf_ref, carry_f, reverse=False)
    _scan_rows(a_ref.at[1], b_ref.at[1], hb_ref, carry_b, reverse=True)


def _rglru(z, rg_conv, wbd, brg, lam):
    t = z.shape[0]
    tm = ROW_TILE
    nchunks = t // tm
    hpt = tm // HALO
    nh = t // HALO

    def bwd(s):
        return jnp.where(s == 0, 0, nchunks - s)

    def prev_map(cidx):
        return lambda s: (jnp.maximum(cidx(s) * hpt - 1, 0), COL_XR)

    def next_map(cidx):
        return lambda s: (jnp.minimum((cidx(s) + 1) * hpt, nh - 1), COL_XR)

    fwd = lambda s: s
    in_specs = [
        pl.BlockSpec((HALO, BRANCH_W), prev_map(fwd)),
        pl.BlockSpec((tm, BRANCH_W), lambda s: (s, COL_XR)),
        pl.BlockSpec((HALO, BRANCH_W), next_map(fwd)),
        pl.BlockSpec((HALO, BRANCH_W), prev_map(bwd)),
        pl.BlockSpec((tm, BRANCH_W), lambda s: (bwd(s), COL_XR)),
        pl.BlockSpec((HALO, BRANCH_W), next_map(bwd)),
        pl.BlockSpec((RG_CONV, BRANCH_W), lambda s: (0, 0)),
        pl.BlockSpec((2, BRANCH_W, 2 * BRANCH_W), lambda s: (0, 0, 0)),
        pl.BlockSpec((2, 1, 2 * BRANCH_W), lambda s: (0, 0, 0)),
        pl.BlockSpec((2, 1, BRANCH_W), lambda s: (0, 0, 0)),
    ]
    return pl.pallas_call(
        functools.partial(_rg_kernel, nchunks=nchunks),
        grid=(nchunks,),
        in_specs=in_specs,
        out_specs=[pl.BlockSpec((tm, BRANCH_W), lambda s: (s, 0)),
                   pl.BlockSpec((tm, BRANCH_W), lambda s: (bwd(s), 0))],
        out_shape=[jax.ShapeDtypeStruct((t, BRANCH_W), F32)] * 2,
        scratch_shapes=[pltpu.VMEM((tm + 2 * HALO, BRANCH_W), F32),
                        pltpu.VMEM((2, BRANCH_W // 128, N_SEG * _seg_pitch(tm // N_SEG), 128), F32),
                        pltpu.VMEM((2, BRANCH_W // 128, N_SEG * _seg_pitch(tm // N_SEG), 128), F32),
                        pltpu.VMEM((8, BRANCH_W), F32),
                        pltpu.VMEM((8, BRANCH_W), F32)],
        compiler_params=_params("arbitrary"),
        name="rglru",
    )(z, z, z, z, z, z, rg_conv, wbd, brg, lam)


def _conf_kernel(vp, vm, vn, gp, gm, gn, dw_ref, lng_ref, lnb_ref, o_ref, ext, shifted, *, nchunks):
    i = pl.program_id(0)
    tm = vm.shape[0]
    prev_ok = (i >= 2).astype(F32)
    next_ok = jnp.logical_and(i >= 1, i <= nchunks - 2).astype(F32)

    def glu(v, g):
        return v[...].astype(F32) * _sigmoid(g[...].astype(F32))

    ext[0:HALO, :] = glu(vp, gp) * prev_ok
    ext[HALO:HALO + tm, :] = glu(vm, gm)
    ext[HALO + tm:2 * HALO + tm, :] = glu(vn, gn) * next_ok
    base = HALO - CONF_WIDTH // 2
    span = tm + 2 * HALO - 8
    for r in range(1, 8):
        shifted[r - 1] = ext[r:r + span, :]
    acc = None
    for k in range(CONF_WIDTH):
        q, r = divmod(base + k, 8)
        rows = ext[8 * q:8 * q + tm, :] if r == 0 else shifted[r - 1, 8 * q:8 * q + tm, :]
        term = dw_ref[k:k + 1, :] * rows
        acc = term if acc is None else acc + term
    mu = jnp.mean(acc, axis=-1, keepdims=True)
    xc = acc - mu
    y = xc * lax.rsqrt(jnp.mean(xc * xc, axis=-1, keepdims=True) + EPS) * lng_ref[...] + lnb_ref[...]
    o_ref[...] = (y * _sigmoid(y)).astype(o_ref.dtype)


def _conformer(z, conf_dw, ln_g, ln_b):
    t = z.shape[0]
    tm = ROW_TILE
    nchunks = t // tm
    hpt = tm // HALO
    nh = t // HALO

    def specs(col):
        return [pl.BlockSpec((HALO, BRANCH_W), lambda i: (jnp.maximum(i * hpt - 1, 0), col)),
                pl.BlockSpec((tm, BRANCH_W), lambda i: (i, col)),
                pl.BlockSpec((HALO, BRANCH_W), lambda i: (jnp.minimum((i + 1) * hpt, nh - 1), col))]

    return pl.pallas_call(
        functools.partial(_conf_kernel, nchunks=nchunks),
        grid=(nchunks,),
        in_specs=specs(COL_CV) + specs(COL_CG) + [
            pl.BlockSpec((CONF_WIDTH, BRANCH_W), lambda i: (0, 0)),
            pl.BlockSpec((1, BRANCH_W), lambda i: (0, 0)),
            pl.BlockSpec((1, BRANCH_W), lambda i: (0, 0))],
        out_specs=pl.BlockSpec((tm, BRANCH_W), lambda i: (i, 0)),
        out_shape=jax.ShapeDtypeStruct((t, BRANCH_W), BF16),
        scratch_shapes=[pltpu.VMEM((tm + 2 * HALO, BRANCH_W), F32),
                        pltpu.VMEM((7, tm + 2 * HALO - 8, BRANCH_W), F32)],
        compiler_params=_params("parallel"),
        name="conformer",
    )(z, z, z, z, z, z, conf_dw, ln_g.reshape(1, -1), ln_b.reshape(1, -1))


def _na_bias_table(rpb):
    i = np.arange(NA_ROWS)[:, None]
    j = np.arange(NA_KEY_ROWS)[None, :]
    variants = [(np.zeros_like(i), i), (i, NA_ROWS + i), (np.full_like(i, NA_ROWS), 2 * NA_ROWS + i)]
    c = np.arange(GRID_W)[:, None]
    kc = np.arange(GRID_W)[None, :]
    col_start = np.clip(c - WIN_COLS // 2, 0, GRID_W - WIN_COLS)
    col_ok = (kc >= col_start) & (kc < col_start + WIN_COLS)
    nh = rpb.shape[0]
    padded = jnp.pad(rpb.astype(F32), ((0, 0), (0, 0), (GRID_W, GRID_W)))
    off = GRID_W + WIN_COLS - 1
    toeplitz = jnp.stack([padded[:, :, off - cc:off - cc + GRID_W] for cc in range(GRID_W)], axis=2)
    toeplitz = jnp.where(col_ok[None, None], toeplitz, NEG)
    masked = jnp.full((nh, GRID_W, GRID_W), NEG, F32)
    tables = [jnp.full((nh, ROW_TILE, NA_KEY_ROWS * GRID_W), NEG, F32)]
    for start, qrow in variants:
        row_ok = (j >= start) & (j < start + WIN_ROWS)
        dr = j - qrow + WIN_ROWS - 1
        rows = [jnp.concatenate([toeplitz[:, dr[qi, kj]] if row_ok[qi, kj] else masked for kj in range(NA_KEY_ROWS)],
                                axis=-1) for qi in range(NA_ROWS)]
        tables.append(jnp.stack(rows, axis=1).reshape(nh, ROW_TILE, NA_KEY_ROWS * GRID_W))
    return jnp.stack(tables)


def _na_kernel(q_ref, k0, k1, k2, kc, v0, v1, v2, vc, bias_ref, o_ref):
    tm = q_ref.shape[0]
    lane = lax.broadcasted_iota(jnp.int32, (1, 128), 1)
    scale = HEAD_DIM ** -0.5
    qmask = [jnp.where(lane < HEAD_DIM, scale, 0.0).astype(BF16), jnp.where(lane >= HEAD_DIM, scale, 0.0).astype(BF16)]
    first_half = lax.broadcasted_iota(jnp.int32, (tm, 128), 1) < HEAD_DIM
    one = jnp.ones((tm, 128), BF16)

    def scores(h):
        cols = slice(128 * (h // 2), 128 * (h // 2 + 1))
        qm = q_ref[:, cols] * qmask[h % 2]
        s = [_dot_nt(qm, r[:, cols]) for r in (k0, k1, k2, kc)]
        return [s[jb] + bias_ref[0, h, :, tm * jb:tm * (jb + 1)] for jb in range(3)] + [s[3]]

    n_heads = BRANCH_W // HEAD_DIM
    s_next = scores(0)
    pair = []
    for h in range(n_heads):
        s = s_next
        if h + 1 < n_heads:
            s_next = scores(h + 1)
        cols = slice(128 * (h // 2), 128 * (h // 2 + 1))
        m = jnp.max(jnp.maximum(jnp.maximum(s[0], s[1]), jnp.maximum(s[2], s[3])), axis=-1, keepdims=True)
        o = None
        for sj, r in zip(s, (v0, v1, v2, vc)):
            va = jnp.where(first_half, r[:, cols], one) if h % 2 == 0 else jnp.where(first_half, one, r[:, cols])
            term = _dot(jnp.exp(sj - m).astype(BF16), va)
            o = term if o is None else o + term
        pair.append(o)
        if h % 2 == 1:
            num = jnp.where(first_half, pair[0], pair[1])
            den = jnp.where(first_half, pltpu.roll(pair[0], HEAD_DIM, 1), pltpu.roll(pair[1], HEAD_DIM, 1))
            o_ref[:, cols] = (num * (1.0 / den)).astype(o_ref.dtype)
            pair = []


def _natten(z, bias):
    t = z.shape[0]
    tm = ROW_TILE
    nt = t // tm
    assert nt >= 4

    def kbase(i):
        return jnp.clip(i - 1, 1, nt - 3)

    def variant(i):
        return jnp.where(i == 0, 0, jnp.where(i == 1, 1, jnp.where(i == nt - 1, 3, 2)))

    def kv_specs(col):
        return [pl.BlockSpec((tm, BRANCH_W), lambda i: (kbase(i), col)),
                pl.BlockSpec((tm, BRANCH_W), lambda i: (kbase(i) + 1, col)),
                pl.BlockSpec((tm, BRANCH_W), lambda i: (kbase(i) + 2, col)),
                pl.BlockSpec((tm, BRANCH_W), lambda i: (0, col))]

    nh = bias.shape[1]
    return pl.pallas_call(
        _na_kernel,
        grid=(nt,),
        in_specs=[pl.BlockSpec((tm, BRANCH_W), lambda i: (i, COL_QA))] + kv_specs(COL_KA) + kv_specs(COL_VA) + [
            pl.BlockSpec((1, nh, tm, 3 * tm), lambda i: (variant(i), 0, 0, 0))],
        out_specs=pl.BlockSpec((tm, BRANCH_W), lambda i: (i, 0)),
        out_shape=jax.ShapeDtypeStruct((t, BRANCH_W), BF16),
        compiler_params=_params("parallel"),
        name="natten",
    )(z, z, z, z, z, z, z, z, z, bias)


def _rope_tables(seq, n_ctx):
    half = HEAD_DIM // 2
    n_f = half // 2
    inv_freq = ROPE_THETA ** (-jnp.arange(n_f, dtype=F32) / n_f)
    pos = jnp.arange(seq)
    ang_r = (pos // GRID_W).astype(F32)[:, None] * inv_freq[None, :]
    ang_c = (pos % GRID_W).astype(F32)[:, None] * inv_freq[None, :]
    cos = jnp.concatenate([jnp.cos(ang_r), jnp.cos(ang_r), jnp.cos(ang_c), jnp.cos(ang_c)], axis=-1)
    sin = jnp.concatenate([-jnp.sin(ang_r), jnp.sin(ang_r), -jnp.sin(ang_c), jnp.sin(ang_c)], axis=-1)
    cos = jnp.concatenate([jnp.ones((n_ctx, HEAD_DIM), F32), cos], axis=0)
    sin = jnp.concatenate([jnp.zeros((n_ctx, HEAD_DIM), F32), sin], axis=0)
    return jnp.tile(cos, (1, 2)), jnp.tile(sin, (1, 2))


def _qkprep_kernel(q_ref, k_ref, v_ref, cos_ref, sin_ref, qg_ref, kg_ref, aq_ref, ak_ref, qo_ref, ko_ref, vo_ref):
    n_f = HEAD_DIM // 4

    def norm_rope(x, g, avg, cos, sin):
        n = x.shape[1]
        x2 = x * x
        hi = x2.astype(BF16)
        lo = (x2 - hi.astype(F32)).astype(BF16)
        ms = _dot(hi, avg) + _dot(lo, avg)
        y = x * lax.rsqrt(ms + EPS) * g
        lane = lax.broadcasted_iota(jnp.int32, x.shape, 1) % (2 * n_f)
        partner = jnp.where(lane < n_f, pltpu.roll(y, n - n_f, 1), pltpu.roll(y, n_f, 1))
        return y * cos + partner * sin

    cos = cos_ref[...]
    sin = sin_ref[...]
    reps = q_ref.shape[1] // 128
    q = norm_rope(q_ref[...].astype(F32), qg_ref[...], aq_ref[...],
                  jnp.concatenate([cos] * reps, axis=1), jnp.concatenate([sin] * reps, axis=1))
    qo_ref[...] = (q * HEAD_DIM ** -0.5).astype(BF16)

    first = lax.broadcasted_iota(jnp.int32, k_ref.shape, 1) < HEAD_DIM
    k = norm_rope(k_ref[...].astype(F32), kg_ref[...], ak_ref[...], cos, sin)
    k_sw = pltpu.roll(k, HEAD_DIM, 1)
    kdup = jnp.concatenate([jnp.where(first, k, k_sw), jnp.where(first, k_sw, k)], axis=1)
    ko_ref[...] = kdup.T.astype(BF16)
    v = v_ref[...].astype(F32)
    v_sw = pltpu.roll(v, HEAD_DIM, 1)
    vo_ref[...] = jnp.concatenate([jnp.where(first, v, 1.0), jnp.where(first, 1.0, v_sw),
                                   jnp.where(first, v_sw, 1.0), jnp.where(first, 1.0, v)], axis=1).astype(BF16)


def _head_avg(n):
    idx = np.arange(n) // HEAD_DIM
    return jnp.asarray((idx[:, None] == idx[None, :]).astype(np.float32) / HEAD_DIM, BF16)


def _qkprep(z, cos, sin, q_norm_g, k_norm_g):
    t = z.shape[0]
    tm = ROW_TILE
    kvw = GQA_KV_HEADS * HEAD_DIM
    qg = jnp.tile(q_norm_g.astype(F32), BRANCH_W // HEAD_DIM).reshape(1, BRANCH_W)
    kg = jnp.tile(k_norm_g.astype(F32), GQA_KV_HEADS).reshape(1, kvw)
    const = lambda i: (0, 0)
    return pl.pallas_call(
        _qkprep_kernel,
        grid=(t // tm,),
        in_specs=[pl.BlockSpec((tm, BRANCH_W), lambda i: (i, COL_QD)),
                  pl.BlockSpec((tm, kvw), lambda i: (i, COL_KD)),
                  pl.BlockSpec((tm, kvw), lambda i: (i, COL_VD)),
                  pl.BlockSpec((tm, kvw), lambda i: (i, 0)),
                  pl.BlockSpec((tm, kvw), lambda i: (i, 0)),
                  pl.BlockSpec((1, BRANCH_W), const),
                  pl.BlockSpec((1, kvw), const),
                  pl.BlockSpec((BRANCH_W, BRANCH_W), const),
                  pl.BlockSpec((kvw, kvw), const)],
        out_specs=[pl.BlockSpec((tm, BRANCH_W), lambda i: (i, 0)),
                   pl.BlockSpec((2 * kvw, tm), lambda i: (0, i)),
                   pl.BlockSpec((tm, 4 * kvw), lambda i: (i, 0))],
        out_shape=[jax.ShapeDtypeStruct((t, BRANCH_W), BF16),
                   jax.ShapeDtypeStruct((2 * kvw, t), BF16),
                   jax.ShapeDtypeStruct((t, 4 * kvw), BF16)],
        compiler_params=_params("parallel"),
        name="qkprep",
    )(z, z, z, cos, sin, qg, kg, _head_avg(BRANCH_W), _head_avg(kvw))


def _gqa_kernel(q_ref, kt_ref, va_ref, o_ref, qs_ref, s_ref, m_ref, acc_ref, *, n_ctx, n_chunks, tk):
    tq = q_ref.shape[0]
    lane = lax.broadcasted_iota(jnp.int32, (1, 128), 1)
    hmask = [(lane < HEAD_DIM).astype(BF16), (lane >= HEAD_DIM).astype(BF16)]
    first_half = lax.broadcasted_iota(jnp.int32, (tq, 128), 1) < HEAD_DIM
    group = BRANCH_W // HEAD_DIM // GQA_KV_HEADS
    for g in range(GQA_KV_HEADS):
        for hh in range(group):
            pair = q_ref[:, 128 * (g * group // 2 + hh // 2):128 * (g * group // 2 + hh // 2 + 1)]
            qs_ref[hh] = pair * hmask[hh % 2]

        def scores(off, size, slot):
            for hh in range(group):
                s_ref[slot, hh, :, 0:size] = _dot(qs_ref[hh], kt_ref[128 * g:128 * (g + 1), pl.ds(off, size)])

        def update(off, size, slot, first):
            for hh in range(group):
                s = s_ref[slot, hh, :, 0:size]
                mx = s[:, 0:128]
                for cc in range(1, size // 128):
                    mx = jnp.maximum(mx, s[:, 128 * cc:128 * (cc + 1)])
                rmax = jnp.max(mx, axis=-1, keepdims=True)
                vcol = 256 * g + 128 * (hh % 2)
                va = va_ref[pl.ds(off, size), vcol:vcol + 128]
                if first:
                    m_new = jnp.broadcast_to(rmax, (tq, 128))
                    p = jnp.exp(s - jnp.concatenate([m_new] * (size // 128), axis=1))
                    acc_ref[hh] = _dot(p.astype(BF16), va)
                else:
                    m_old = m_ref[hh]
                    m_new = jnp.maximum(m_old, rmax)
                    p = jnp.exp(s - jnp.concatenate([m_new] * (size // 128), axis=1))
                    acc_ref[hh] = jnp.exp(m_old - m_new) * acc_ref[hh] + _dot(p.astype(BF16), va)
                m_ref[hh] = m_new

        scores(0, n_ctx, 1)
        if n_chunks:
            scores(n_ctx, tk, 0)
        update(0, n_ctx, 1, True)

        def body(c2, carry):
            off = pl.multiple_of(n_ctx + 2 * c2 * tk, 256)
            scores(off + tk, tk, 1)
            update(off, tk, 0, False)
            scores(off + 2 * tk, tk, 0)
            update(off + tk, tk, 1, False)
            return carry

        if n_chunks:
            lax.fori_loop(0, n_chunks // 2 - 1, body, 0)
            off = n_ctx + (n_chunks - 2) * tk
            scores(off + tk, tk, 1)
            update(off, tk, 0, False)
            update(off + tk, tk, 1, False)

        for pr in range(group // 2):
            o0 = acc_ref[2 * pr]
            o1 = acc_ref[2 * pr + 1]
            num = jnp.where(first_half, o0, o1)
            den = jnp.where(first_half, pltpu.roll(o0, HEAD_DIM, 1), pltpu.roll(o1, HEAD_DIM, 1))
            col = 128 * (g * group // 2 + pr)
            o_ref[:, col:col + 128] = (num * (1.0 / den)).astype(o_ref.dtype)


def _gqa(qn, kt, va, n_ctx):
    t = qn.shape[0]
    seq = t - n_ctx
    tq = _pick(seq, (512, 256))
    tk = _pick(seq, (2048, 1024, 512)) // 2
    assert tk >= n_ctx
    group = BRANCH_W // HEAD_DIM // GQA_KV_HEADS
    resident = [pl.BlockSpec(kt.shape, lambda i: (0, 0), pipeline_mode=pl.Buffered(1)),
                pl.BlockSpec(va.shape, lambda i: (0, 0), pipeline_mode=pl.Buffered(1))]

    def scratch(rows, width):
        return [pltpu.VMEM((group, rows, 128), BF16),
                pltpu.VMEM((2, group, rows, width), F32),
                pltpu.VMEM((group, rows, 128), F32),
                pltpu.VMEM((group, rows, 128), F32)]

    out_ctx = pl.pallas_call(
        functools.partial(_gqa_kernel, n_ctx=n_ctx, n_chunks=0, tk=tk),
        grid=(1,),
        in_specs=[pl.BlockSpec((n_ctx, BRANCH_W), lambda i: (0, 0))] + resident,
        out_specs=pl.BlockSpec((n_ctx, BRANCH_W), lambda i: (0, 0)),
        out_shape=jax.ShapeDtypeStruct((n_ctx, BRANCH_W), BF16),
        scratch_shapes=scratch(n_ctx, n_ctx),
        compiler_params=_params("arbitrary"),
        name="gqa_ctx",
    )(qn, kt, va)
    out_lat = pl.pallas_call(
        functools.partial(_gqa_kernel, n_ctx=n_ctx, n_chunks=seq // tk, tk=tk),
        grid=(seq // tq,),
        in_specs=[pl.BlockSpec((pl.Element(tq), pl.Element(BRANCH_W)),
                               lambda i: (pl.multiple_of(n_ctx + i * tq, ROW_TILE), 0))] + resident,
        out_specs=pl.BlockSpec((tq, BRANCH_W), lambda i: (i, 0)),
        out_shape=jax.ShapeDtypeStruct((seq, BRANCH_W), BF16),
        scratch_shapes=scratch(tq, tk),
        compiler_params=_params("parallel"),
        name="gqa",
    )(qn, kt, va)
    return out_ctx, out_lat


def _gelu_tanh(x):
    return 0.5 * x * (1.0 + jnp.tanh(np.sqrt(2.0 / np.pi).astype(np.float32) * (x + 0.044715 * (x * x * x))))


def _merge_kernel(a_ref, hf_ref, hb_ref, gr_ref, c_ref, dc_ref, dl_ref, g0, g1, g2, g3, x_ref, mod_ref, wb_ref, wo_ref,
                  o_ref, *, nctx_tiles):
    b = ((hf_ref[...] + hb_ref[...]) * _gelu_tanh(gr_ref[...].astype(F32))).astype(BF16)
    d = jnp.where(pl.program_id(0) < nctx_tiles, dc_ref[...], dl_ref[...])
    branches = (a_ref[...], b, c_ref[...], d)
    gates = (g0, g1, g2, g3)
    y = None
    for n in range(N_BRANCH):
        term = _sigmoid(gates[n][...].astype(F32)) * _dot(branches[n], wb_ref[n])
        y = term if y is None else y + term
    out = _dot(y.astype(BF16), wo_ref[...])
    o_ref[...] = x_ref[...] + mod_ref[0, 0, 2:3, :] * out


def _merge(a, hf, hb, z, c, d_ctx, d_lat, x, mod, layer, wb, wo, n_ctx):
    t, d = x.shape
    tm = ROW_TILE
    nctx_tiles = n_ctx // tm
    assert nctx_tiles == 1
    row = lambda i: (i, 0)
    single = pl.Buffered(1)
    return pl.pallas_call(
        functools.partial(_merge_kernel, nctx_tiles=nctx_tiles),
        grid=(t // tm,),
        in_specs=[pl.BlockSpec((tm, BRANCH_W), row),
                  pl.BlockSpec((tm, BRANCH_W), row),
                  pl.BlockSpec((tm, BRANCH_W), row),
                  pl.BlockSpec((tm, BRANCH_W), lambda i: (i, COL_GR)),
                  pl.BlockSpec((tm, BRANCH_W), row),
                  pl.BlockSpec((tm, BRANCH_W), lambda i: (0, 0)),
                  pl.BlockSpec((tm, BRANCH_W), lambda i: (jnp.maximum(i - nctx_tiles, 0), 0))] + [
                      pl.BlockSpec((tm, d), functools.partial(lambda n, i: (i, n), n)) for n in range(N_BRANCH)] + [
                  pl.BlockSpec((tm, d), row),
                  pl.BlockSpec((1, 1, 8, d), lambda i: (layer, (i < nctx_tiles).astype(jnp.int32), 0, 0)),
                  pl.BlockSpec((None, N_BRANCH, BRANCH_W, d), lambda i: (layer, 0, 0, 0), pipeline_mode=single),
                  pl.BlockSpec((None, d, d), lambda i: (layer, 0, 0), pipeline_mode=single)],
        out_specs=pl.BlockSpec((tm, d), row),
        out_shape=jax.ShapeDtypeStruct((t, d), F32),
        compiler_params=_params("parallel"),
        name="merge",
    )(a, hf, hb, z, c, d_ctx, d_lat, z, z, z, z, x, mod, wb, wo)


def _ffn_up_kernel(xp_ref, x_ref, xn_ref, g_ref, mod_ref, wg_ref, wv_ref, dwg_ref, dwv_ref, o_ref,
                   h_ref, ug_ref, uv_ref, *, tm, n_ctx):
    i = pl.program_id(0)

    @pl.when(pl.program_id(1) == 0)
    def _():
        prev_ok = (i > 0).astype(F32)
        next_ok = (i < pl.num_programs(0) - 1).astype(F32)
        _norm_modulate(xp_ref, h_ref, 0, g_ref, mod_ref, 3, i * tm - HALO, n_ctx, keep=prev_ok)
        _norm_modulate(x_ref, h_ref, HALO, g_ref, mod_ref, 3, i * tm, n_ctx)
        _norm_modulate(xn_ref, h_ref, HALO + tm, g_ref, mod_ref, 3, (i + 1) * tm, n_ctx, keep=next_ok)

    rows = i * tm + lax.broadcasted_iota(jnp.int32, (tm, 1), 0)
    keep_prev = (rows != n_ctx).astype(F32)
    keep_next = (rows != n_ctx - 1).astype(F32)

    def conv(u_ref, w_ref, dw_ref):
        u_ref[...] = _dot(h_ref[...], w_ref[...])
        prev = u_ref[HALO - 1:HALO - 1 + tm, :] * keep_prev
        nxt = u_ref[HALO + 1:HALO + 1 + tm, :] * keep_next
        return dw_ref[0:1, :] * prev + dw_ref[1:2, :] * u_ref[HALO:HALO + tm, :] + dw_ref[2:3, :] * nxt

    ug = conv(ug_ref, wg_ref, dwg_ref)
    uv = conv(uv_ref, wv_ref, dwv_ref)
    hg = 0.5 * ug
    o_ref[...] = ((hg + hg * jnp.tanh(hg)) * uv).astype(o_ref.dtype)


def _ffn_up(x, g, mod, layer, w_up, ffn_dw, n_ctx):
    t, d = x.shape
    dff = w_up.shape[2] // 2
    tm = _pick(t, (768, 256))
    tn = _pick(dff, (512, 256))
    nj = dff // tn
    hpt = tm // HALO
    nh = t // HALO
    return pl.pallas_call(
        functools.partial(_ffn_up_kernel, tm=tm, n_ctx=n_ctx),
        grid=(t // tm, nj),
        in_specs=[pl.BlockSpec((HALO, d), lambda i, j: (jnp.maximum(i * hpt - 1, 0), 0)),
                  pl.BlockSpec((tm, d), lambda i, j: (i, 0)),
                  pl.BlockSpec((HALO, d), lambda i, j: (jnp.minimum((i + 1) * hpt, nh - 1), 0)),
                  pl.BlockSpec((1, d), lambda i, j: (0, 0)),
                  pl.BlockSpec((1, 2, 8, d), lambda i, j: (layer, 0, 0, 0)),
                  pl.BlockSpec((None, d, tn), lambda i, j: (layer, 0, j)),
                  pl.BlockSpec((None, d, tn), lambda i, j: (layer, 0, j + nj)),
                  pl.BlockSpec((3, tn), lambda i, j: (0, j)),
                  pl.BlockSpec((3, tn), lambda i, j: (0, j + nj))],
        out_specs=pl.BlockSpec((tm, tn), lambda i, j: (i, j)),
        out_shape=jax.ShapeDtypeStruct((t, dff), BF16),
        scratch_shapes=[pltpu.VMEM((tm + 2 * HALO, d), BF16),
                        pltpu.VMEM((tm + 2 * HALO, tn), F32),
                        pltpu.VMEM((tm + 2 * HALO, tn), F32)],
        compiler_params=_params("parallel", "arbitrary"),
        name="ffn_up",
    )(x, x, x, g.reshape(1, d), mod, w_up, w_up, ffn_dw, ffn_dw)


def _ffn_down_kernel(a_ref, w_ref, x_ref, mod_ref, o_ref, *, tm, n_ctx):
    rows = pl.program_id(0) * tm + lax.broadcasted_iota(jnp.int32, (tm, 1), 0)
    gate = jnp.where(rows < n_ctx, mod_ref[0, 1, 5:6, :], mod_ref[0, 0, 5:6, :])
    o_ref[...] = x_ref[...] + gate * _dot(a_ref[...], w_ref[...])


def _ffn_down(act, wd, x, mod, layer, n_ctx):
    t, d = x.shape
    dff = wd.shape[1]
    tm = _pick(t, (768, 256))
    tn = 512
    return pl.pallas_call(
        functools.partial(_ffn_down_kernel, tm=tm, n_ctx=n_ctx),
        grid=(t // tm, d // tn),
        in_specs=[pl.BlockSpec((tm, dff), lambda i, j: (i, 0)),
                  pl.BlockSpec((None, dff, tn), lambda i, j: (layer, 0, j)),
                  pl.BlockSpec((tm, tn), lambda i, j: (i, j)),
                  pl.BlockSpec((1, 2, 8, tn), lambda i, j: (layer, 0, 0, j))],
        out_specs=pl.BlockSpec((tm, tn), lambda i, j: (i, j)),
        out_shape=jax.ShapeDtypeStruct((t, d), F32),
        compiler_params=_params("parallel", "arbitrary"),
        name="ffn_down",
    )(act, wd, x, mod)


def _final_kernel(x_ref, g_ref, o_ref):
    x = x_ref[...]
    o_ref[...] = x * lax.rsqrt(jnp.mean(x * x, axis=-1, keepdims=True) + EPS) * g_ref[...]


def _final_norm(x, g, n_ctx):
    t, d = x.shape
    tm = ROW_TILE
    skip = n_ctx // tm
    return pl.pallas_call(
        _final_kernel,
        grid=((t - n_ctx) // tm,),
        in_specs=[pl.BlockSpec((tm, d), lambda i: (i + skip, 0)),
                  pl.BlockSpec((1, d), lambda i: (0, 0))],
        out_specs=pl.BlockSpec((tm, d), lambda i: (i, 0)),
        out_shape=jax.ShapeDtypeStruct((t - n_ctx, d), F32),
        compiler_params=_params("parallel"),
        name="final_norm",
    )(x, g.reshape(1, d))


def _block_diag(w):
    nb, bi, bj = w.shape
    eye = jnp.eye(nb, dtype=w.dtype)
    return jnp.einsum('nij,nm->nimj', w, eye).reshape(nb * bi, nb * bj)


def kernel(x, c, ctx, c_ctx, w_ada, b_ada, g_mix, w_in, na_rpb, rg_conv, w_rg, b_rg, rg_lambda, conf_dw, conf_ln_g,
           conf_ln_b, q_norm_g, k_norm_g, w_branch, w_out, g_ffn, w_up, ffn_dw, w_down, g_final):
    bsz, seq, d = x.shape
    n_ctx = ctx.shape[1]
    assert bsz == 1 and d == D_MODEL and n_ctx == ROW_TILE and seq % ROW_TILE == 0
    assert seq // GRID_W >= NA_KEY_ROWS

    cc = jnp.concatenate([c, c_ctx[None, :], jnp.zeros((6, d), F32)], axis=0)
    mod = _ada(cc, w_ada, b_ada)[:, :2, :].reshape(DEPTH, 2, N_MOD, d)
    mod = jnp.pad(mod, ((0, 0), (0, 0), (0, 8 - N_MOD), (0, 0)))

    xs = jnp.concatenate([ctx[0], x[0]], axis=0)
    cos, sin = _rope_tables(seq, n_ctx)

    w_in_b = _w_in_gates_first(w_in)
    w_branch_b, w_out_b, w_up_b, w_down_b = (w.astype(BF16) for w in (w_branch, w_out, w_up, w_down))

    for l in range(DEPTH):
        z = _proj(xs, g_mix[l], mod, l, 0, w_in_b, n_ctx, (1792, 896, 256))

        wbd = jnp.stack([jnp.concatenate([_block_diag(w_rg[l, dd, 0]), _block_diag(w_rg[l, dd, 1])], axis=1)
                         for dd in range(2)]).astype(BF16)
        brg = b_rg[l].reshape(2, 1, 2 * BRANCH_W)
        hf, hb = _rglru(z, rg_conv[l], wbd, brg, rg_lambda[l].reshape(2, 1, BRANCH_W))
        c_br = _conformer(z, conf_dw[l], conf_ln_g[l], conf_ln_b[l])
        a_br = _natten(z, _na_bias_table(na_rpb[l]))
        qn, kt, va = _qkprep(z, cos, sin, q_norm_g[l], k_norm_g[l])
        d_ctx, d_lat = _gqa(qn, kt, va, n_ctx)
        xs = _merge(a_br, hf, hb, z, c_br, d_ctx, d_lat, xs, mod, l, w_branch_b, w_out_b, n_ctx)

        act = _ffn_up(xs, g_ffn[l], mod, l, w_up_b, ffn_dw[l], n_ctx)
        xs = _ffn_down(act, w_down_b, xs, mod, l, n_ctx)

    return _final_norm(xs, g_final, n_ctx)[None]
```
